```python
import jax, jax.numpy as jnp
from jax import lax
import numpy as np

D_MODEL = 4096
BATCH = 4
SEQ = 2048
DEPTH = 1
DEC_BATCH = 128
DEC_SEQ = 8
PAST_LEN = 2048
PAGE_SIZE = 128

RW_HEAD = 64
RW_WIDTH = D_MODEL // 2
RW_HEADS = RW_WIDTH // RW_HEAD
RW_R_DECAY = max(32, int(round(1.8 * RW_WIDTH ** 0.5 / 32)) * 32)
RW_R_AAA = max(32, int(round(1.8 * RW_WIDTH ** 0.5 / 32)) * 32)
RW_R_GATE = max(32, int(round(0.6 * RW_WIDTH ** 0.8 / 32)) * 32)
RW_COLS = 3 * RW_WIDTH + RW_R_DECAY + RW_R_AAA + RW_R_GATE
GN_EPS = 64e-5

ATT_GROUPS = ((128, 1), (512, 4), (2048, 16))
N_GROUPS = 3
ATT_HEADS = D_MODEL // 512
ATT_HEAD_DIM = 128
ATT_WIDTH = ATT_HEADS * ATT_HEAD_DIM
ATT_COLS = N_GROUPS * 3 * ATT_WIDTH

GATE_COLS = 2 * D_MODEL
N_IN = ATT_COLS + RW_COLS + GATE_COLS
D_FF = ((8 * D_MODEL + 3 * 256 - 1) // (3 * 256)) * 256
RMS_EPS = 1e-6

kernel_name = 'hybrid_rwkv7_dilated_alibi_step'

F32 = jnp.float32


def rms_norm(x, g):
    xf = x.astype(F32)
    return (xf * lax.rsqrt(jnp.mean(xf * xf, -1, keepdims=True) + RMS_EPS)).astype(x.dtype) * g


def alibi_slopes():
    n = N_GROUPS * ATT_HEADS
    return 2.0 ** (-8.0 * jnp.arange(1, n + 1, dtype=F32) / n)


def dilated_attn_prompt(q, k, v, window, dil, slopes):
    B, T, H, E = q.shape
    n = window // dil
    span = n * dil
    Tp = -(-T // span) * span
    M = Tp // dil
    NB = M // n

    def to_blocks(a):
        a = jnp.pad(a, ((0, 0), (0, Tp - T), (0, 0), (0, 0))).reshape(B, M, dil, H, E)
        return a.transpose(0, 2, 1, 3, 4).reshape(B, dil, NB, n, H, E)

    def with_prev(a):
        prev = jnp.pad(a[:, :, :-1], ((0, 0), (0, 0), (1, 0), (0, 0), (0, 0), (0, 0)))
        return jnp.concatenate([prev, a], axis=3)

    qb = to_blocks(q)
    kb = with_prev(to_blocks(k))
    vb = with_prev(to_blocks(v))
    s = jnp.einsum('brnqhe,brnkhe->brnhqk', qb, kb, preferred_element_type=F32) * (E ** -0.5)
    delta = n + jnp.arange(n)[:, None] - jnp.arange(2 * n)[None, :]
    valid = (delta >= 0) & (delta <= n)
    valid = valid[None] & ~((jnp.arange(NB) == 0)[:, None, None] & (jnp.arange(2 * n) < n)[None, None, :])
    bias = -slopes[:, None, None] * (delta * dil).astype(F32)[None]
    s = jnp.where(valid[None, None, :, None], s + bias[None, None, None], -jnp.inf)
    m = jnp.max(s, -1, keepdims=True)
    p = jnp.exp(s - m)
    l = jnp.sum(p, -1, keepdims=True)
    o = jnp.einsum('brnhqk,brnkhe->brnhqe', p, vb.astype(F32)) / l
    lse = (m + jnp.log(l))[..., 0]
    o = o.transpose(0, 2, 4, 1, 3, 5).reshape(B, Tp, H, E)[:, :T]
    lse = lse.transpose(0, 2, 4, 1, 3).reshape(B, Tp, H)[:, :T]
    return o, lse


def dilated_attn_sample(q, k, v, kv_buf, window, dil, slopes):
    S = q.shape[1]
    L = kv_buf.shape[1]
    n = window // dil
    E = q.shape[-1]
    k_all = jnp.concatenate([kv_buf[:, :, 0].astype(k.dtype), k], axis=1)
    v_all = jnp.concatenate([kv_buf[:, :, 1].astype(v.dtype), v], axis=1)
    j = jnp.arange(n + 1)
    idx = L + jnp.arange(S)[:, None] - j[None, :] * dil
    valid = idx >= 0
    idx = jnp.maximum(idx, 0)
    kg = k_all[:, idx]
    vg = v_all[:, idx]
    s = jnp.einsum('bshe,bsjhe->bhsj', q, kg, preferred_element_type=F32) * (E ** -0.5)
    s = s - slopes[:, None, None] * (j * dil).astype(F32)[None, None, :]
    s = jnp.where(valid[None, None], s, -jnp.inf)
    m = jnp.max(s, -1, keepdims=True)
    p = jnp.exp(s - m)
    l = jnp.sum(p, -1, keepdims=True)
    o = jnp.einsum('bhsj,bsjhe->bshe', p, vg.astype(F32)) / l.transpose(0, 2, 1, 3)
    lse = (m + jnp.log(l))[..., 0].transpose(0, 2, 1)
    return o, lse


def rwkv_time_mix(cols, prev_row, wkv0, lp):
    B, T, _ = cols.shape
    C = RW_WIDTH
    prev = jnp.concatenate([prev_row[:, None, :].astype(cols.dtype), cols[:, :-1]], axis=1)
    xs = cols + (prev - cols) * lp['rw_mu']
    r = xs[..., :C]
    k = xs[..., C:2 * C]
    v = xs[..., 2 * C:3 * C]
    o1 = 3 * C
    wd = xs[..., o1:o1 + RW_R_DECAY]
    o2 = o1 + RW_R_DECAY
    ad = xs[..., o2:o2 + RW_R_AAA]
    gd = xs[..., o2 + RW_R_AAA:]
    w_log = -jax.nn.softplus(-(lp['rw_w0'] + jnp.tanh(wd) @ lp['rw_w_up']).astype(F32)) - 0.5
    decay = jnp.exp(-jnp.exp(w_log))
    a = jax.nn.sigmoid((lp['rw_a0'] + ad @ lp['rw_a_up']).astype(F32))
    g = jax.nn.sigmoid(gd) @ lp['rw_g_up']
    heads = lambda t: t.astype(F32).reshape(B, T, RW_HEADS, RW_HEAD)
    kk = heads(k * lp['rw_k_k'])
    kk = kk * lax.rsqrt(jnp.maximum(jnp.sum(kk * kk, -1, keepdims=True), 1e-24))
    k_mod = k.astype(F32) * (1.0 + (a - 1.0) * lp['rw_k_a'].astype(F32))
    rh, kh, vh, wh = heads(r), heads(k_mod), heads(v), heads(decay)
    bh = kk * heads(a)

    def step(S, inp):
        r_t, w_t, k_t, v_t, kk_t, b_t = inp
        sa = jnp.einsum('bhij,bhj->bhi', S, -kk_t)
        S = S * w_t[:, :, None, :] + sa[..., None] * b_t[:, :, None, :] + v_t[..., None] * k_t[:, :, None, :]
        return S, jnp.einsum('bhij,bhj->bhi', S, r_t)

    seq_first = lambda t: jnp.swapaxes(t, 0, 1)
    wkv_T, ys = lax.scan(step, wkv0.astype(F32), tuple(map(seq_first, (rh, wh, kh, vh, kk, bh))))
    y = seq_first(ys)
    mu = jnp.mean(y, -1, keepdims=True)
    var = jnp.mean(jnp.square(y - mu), -1, keepdims=True)
    y = (y - mu) * lax.rsqrt(var + GN_EPS)
    bonus = jnp.sum(rh * kh * lp['rw_r_k'].astype(F32), -1, keepdims=True) * vh
    y = (y.reshape(B, T, C) * lp['rw_lnx_w'] + lp['rw_lnx_b'] + bonus.reshape(B, T, C)) * g
    return y.astype(cols.dtype) @ lp['rw_out'], wkv_T, cols[:, -1]


def hybrid_layer(x, lp, shift_row, wkv0, kv_bufs):
    B, T, _ = x.shape
    h = rms_norm(x, lp['norm_mix'])
    z = h @ lp['w_in']
    qkv = z[..., :ATT_COLS].reshape(B, T, N_GROUPS, 3, ATT_HEADS, ATT_HEAD_DIM)
    rw_cols = z[..., ATT_COLS:ATT_COLS + RW_COLS]
    gates = jax.nn.sigmoid(z[..., ATT_COLS + RW_COLS:].astype(F32))
    slopes = alibi_slopes()
    outs, lses, new_kv = [], [], []
    for gi, (win, dil) in enumerate(ATT_GROUPS):
        q, k, v = qkv[:, :, gi, 0], qkv[:, :, gi, 1], qkv[:, :, gi, 2]
        sl = slopes[gi * ATT_HEADS:(gi + 1) * ATT_HEADS]
        kv = jnp.stack([k, v], axis=2)
        if kv_bufs is None:
            o, lse = dilated_attn_prompt(q, k, v, win, dil, sl)
            new_kv.append(kv[:, T - min(win, T):])
        else:
            o, lse = dilated_attn_sample(q, k, v, kv_bufs[gi], win, dil, sl)
            new_kv.append(kv)
        outs.append(o)
        lses.append(lse)
    wts = jax.nn.softmax(jnp.stack(lses), axis=0)
    att = jnp.einsum('gbth,gbthe->bthe', wts, jnp.stack(outs))
    att = att.reshape(B, T, ATT_WIDTH).astype(x.dtype) @ lp['att_out']
    rw, wkv_T, last = rwkv_time_mix(rw_cols, shift_row, wkv0, lp)
    mixed = (gates[..., :D_MODEL] * rw + gates[..., D_MODEL:] * att).astype(x.dtype)
    x = x + mixed @ lp['w_o']
    hf = rms_norm(x, lp['norm_ffn'])
    gu = hf @ lp['w_ffn_in']
    x = x + (jax.nn.silu(gu[..., :D_FF]) * gu[..., D_FF:]) @ lp['w_ffn_out']
    return x, new_kv, wkv_T, last


def setup_inputs(seed: int = 0) -> dict:
    key = jax.random.key(seed)
    ks = iter(jax.random.split(key, 40))
    nrm = lambda shape, scale: scale * jax.random.normal(next(ks), shape, F32)
    uni = lambda shape, lo, hi: jax.random.uniform(next(ks), shape, F32, lo, hi)
    kv_shape = lambda w: (DEPTH, DEC_BATCH, min(w, PAST_LEN), 2, ATT_HEADS, ATT_HEAD_DIM)
    return {
        'x_prompt': nrm((BATCH, SEQ, D_MODEL), 1.0),
        'x_sample': nrm((DEC_BATCH, DEC_SEQ, D_MODEL), 1.0),
        'cache_kv_0': nrm(kv_shape(ATT_GROUPS[0][0]), 1.0),
        'cache_kv_1': nrm(kv_shape(ATT_GROUPS[1][0]), 1.0),
        'cache_kv_2': nrm(kv_shape(ATT_GROUPS[2][0]), 1.0),
        'state_wkv': nrm((DEPTH, DEC_BATCH, RW_HEADS, RW_HEAD, RW_HEAD), 0.5),
        'state_shift': nrm((DEPTH, DEC_BATCH, RW_COLS), 1.0),
        'norm_mix': 1.0 + nrm((DEPTH, D_MODEL), 0.02),
        'w_in': nrm((DEPTH, D_MODEL, N_IN), D_MODEL ** -0.5),
        'rw_mu': uni((DEPTH, RW_COLS), 0.0, 1.0),
        'rw_w0': uni((DEPTH, RW_WIDTH), -6.0, -1.0),
        'rw_w_up': nrm((DEPTH, RW_R_DECAY, RW_WIDTH), 0.1 * RW_R_DECAY ** -0.5),
        'rw_a0': nrm((DEPTH, RW_WIDTH), 0.1),
        'rw_a_up': nrm((DEPTH, RW_R_AAA, RW_WIDTH), 0.5 * RW_R_AAA ** -0.5),
        'rw_g_up': nrm((DEPTH, RW_R_GATE, RW_WIDTH), RW_R_GATE ** -0.5),
        'rw_k_k': 0.85 + nrm((DEPTH, RW_WIDTH), 0.05),
        'rw_k_a': 1.0 + nrm((DEPTH, RW_WIDTH), 0.05),
        'rw_r_k': nrm((DEPTH, RW_HEADS, RW_HEAD), 0.1),
        'rw_lnx_w': 1.0 + nrm((DEPTH, RW_WIDTH), 0.02),
        'rw_lnx_b': nrm((DEPTH, RW_WIDTH), 0.02),
        'rw_out': nrm((DEPTH, RW_WIDTH, D_MODEL), RW_WIDTH ** -0.5),
        'att_out': nrm((DEPTH, ATT_WIDTH, D_MODEL), ATT_WIDTH ** -0.5),
        'w_o': nrm((DEPTH, D_MODEL, D_MODEL), D_MODEL ** -0.5),
        'norm_ffn': 1.0 + nrm((DEPTH, D_MODEL), 0.02),
        'w_ffn_in': nrm((DEPTH, D_MODEL, 2 * D_FF), D_MODEL ** -0.5),
        'w_ffn_out': nrm((DEPTH, D_FF, D_MODEL), D_FF ** -0.5),
        'norm_final': 1.0 + nrm((D_MODEL,), 0.02),
    }


def reference(x_prompt, x_sample, cache_kv_0, cache_kv_1, cache_kv_2, state_wkv, state_shift,
              norm_mix, w_in, rw_mu, rw_w0, rw_w_up, rw_a0, rw_a_up, rw_g_up, rw_k_k, rw_k_a,
              rw_r_k, rw_lnx_w, rw_lnx_b, rw_out, att_out, w_o, norm_ffn, w_ffn_in, w_ffn_out,
              norm_final):
    xp, xs = x_prompt, x_sample
    bp = xp.shape[0]
    kv_p, kv_s = ([], [], []), ([], [], [])
    wkv_p_l, wkv_s_l, sh_p_l, sh_s_l = [], [], [], []
    for l in range(DEPTH):
        lp = {
            'norm_mix': norm_mix[l], 'w_in': w_in[l], 'rw_mu': rw_mu[l], 'rw_w0': rw_w0[l],
            'rw_w_up': rw_w_up[l], 'rw_a0': rw_a0[l], 'rw_a_up': rw_a_up[l], 'rw_g_up': rw_g_up[l],
            'rw_k_k': rw_k_k[l], 'rw_k_a': rw_k_a[l], 'rw_r_k': rw_r_k[l], 'rw_lnx_w': rw_lnx_w[l],
            'rw_lnx_b': rw_lnx_b[l], 'rw_out': rw_out[l], 'att_out': att_out[l], 'w_o': w_o[l],
            'norm_ffn': norm_ffn[l], 'w_ffn_in': w_ffn_in[l], 'w_ffn_out': w_ffn_out[l],
        }
        xp, nkv_p, wkv_p, sh_p = hybrid_layer(
            xp, lp, jnp.zeros((bp, RW_COLS), xp.dtype),
            jnp.zeros((bp, RW_HEADS, RW_HEAD, RW_HEAD), F32), None)
        xs, nkv_s, wkv_s, sh_s = hybrid_layer(
            xs, lp, state_shift[l], state_wkv[l], (cache_kv_0[l], cache_kv_1[l], cache_kv_2[l]))
        for gi in range(N_GROUPS):
            kv_p[gi].append(nkv_p[gi])
            kv_s[gi].append(nkv_s[gi])
        wkv_p_l.append(wkv_p)
        wkv_s_l.append(wkv_s)
        sh_p_l.append(sh_p)
        sh_s_l.append(sh_s)
    y_prompt = rms_norm(xp, norm_final)
    y_sample = rms_norm(xs, norm_final)
    kv0_p, kv1_p, kv2_p = jnp.stack(kv_p[0]), jnp.stack(kv_p[1]), jnp.stack(kv_p[2])
    kv0_s, kv1_s, kv2_s = jnp.stack(kv_s[0]), jnp.stack(kv_s[1]), jnp.stack(kv_s[2])
    wkv_p_all, wkv_s_all = jnp.stack(wkv_p_l), jnp.stack(wkv_s_l)
    sh_p_all, sh_s_all = jnp.stack(sh_p_l), jnp.stack(sh_s_l)
    return (y_prompt, y_sample, kv0_p, kv1_p, kv2_p, wkv_p_all, sh_p_all,
            kv0_s, kv1_s, kv2_s, wkv_s_all, sh_s_all)
```

```python
import functools

import jax
import jax.numpy as jnp
from jax import lax
from jax.experimental import pallas as pl
from jax.experimental.pallas import tpu as pltpu

F32 = jnp.float32
BF16 = jnp.bfloat16

RMS_EPS = 1e-6
GN_EPS = 64e-5
RW_HEAD = 64
ATT_GROUPS = ((128, 1), (512, 4), (2048, 16))
ATT_HEAD_DIM = 128
ATT_HEADS = 8
LANES = 128
SUBLANES = 8
NEG_INF = float("-inf")


def _params(sem, vmem_mib):
    return pltpu.CompilerParams(dimension_semantics=sem, vmem_limit_bytes=vmem_mib << 20)


def _rmsnorm_kernel(x_ref, g_ref, o_ref):
    x = x_ref[...]
    ms = jnp.mean(x * x, axis=-1, keepdims=True)
    o_ref[...] = (x * lax.rsqrt(ms + RMS_EPS) * g_ref[...]).astype(o_ref.dtype)


def _rmsnorm(x, g, out_dtype, tm=256):
    m, d = x.shape
    return pl.pallas_call(
        _rmsnorm_kernel,
        grid=(m // tm,),
        in_specs=[pl.BlockSpec((tm, d), lambda i: (i, 0)), pl.BlockSpec((1, d), lambda i: (0, 0))],
        out_specs=pl.BlockSpec((tm, d), lambda i: (i, 0)),
        out_shape=jax.ShapeDtypeStruct((m, d), out_dtype),
        compiler_params=_params(("parallel",), 32),
        name="rmsnorm",
    )(x, g.reshape(1, d))


def _mm_kernel(a_ref, w_ref, o_ref):
    o_ref[...] = jnp.dot(a_ref[...], w_ref[...], preferred_element_type=F32).astype(o_ref.dtype)


def _mm_res_kernel(a_ref, w_ref, r_ref, o_ref):
    o_ref[...] = r_ref[...] + jnp.dot(a_ref[...], w_ref[...], preferred_element_type=F32)


def _matmul(a, w, *, tm, tn, residual=None, vmem_mib=48, name="matmul"):
    m, k = a.shape
    n = w.shape[1]
    in_specs = [pl.BlockSpec((tm, k), lambda i, j: (i, 0)), pl.BlockSpec((k, tn), lambda i, j: (0, j))]
    args = [a, w]
    kern = _mm_kernel
    if residual is not None:
        in_specs.append(pl.BlockSpec((tm, tn), lambda i, j: (i, j)))
        args.append(residual)
        kern = _mm_res_kernel
    return pl.pallas_call(
        kern,
        grid=(m // tm, n // tn),
        in_specs=in_specs,
        out_specs=pl.BlockSpec((tm, tn), lambda i, j: (i, j)),
        out_shape=jax.ShapeDtypeStruct((m, n), F32),
        compiler_params=_params(("parallel", "arbitrary"), vmem_mib),
        name=name,
    )(*args)


def _swiglu_kernel(a_ref, wg_ref, wu_ref, o_ref):
    a = a_ref[...]
    g = jnp.dot(a, wg_ref[...], preferred_element_type=F32)
    u = jnp.dot(a, wu_ref[...], preferred_element_type=F32)
    o_ref[...] = (g * jax.nn.sigmoid(g) * u).astype(o_ref.dtype)


def _swiglu(a, w, *, tm, tn):
    m, k = a.shape
    f = w.shape[1] // 2
    nb = f // tn
    return pl.pallas_call(
        _swiglu_kernel,
        grid=(m // tm, nb),
        in_specs=[
            pl.BlockSpec((tm, k), lambda i, j: (i, 0)),
            pl.BlockSpec((k, tn), lambda i, j: (0, j)),
            pl.BlockSpec((k, tn), lambda i, j: (0, j + nb)),
        ],
        out_specs=pl.BlockSpec((tm, tn), lambda i, j: (i, j)),
        out_shape=jax.ShapeDtypeStruct((m, f), BF16),
        compiler_params=_params(("parallel", "arbitrary"), 48),
        name="swiglu",
    )(a, w, w)


def _mix_kernel(rw_ref, at_ref, wr_ref, wa_ref, ga_ref, gb_ref, o_ref):
    rw = jnp.dot(rw_ref[...], wr_ref[...], preferred_element_type=F32)
    at = jnp.dot(at_ref[...], wa_ref[...], preferred_element_type=F32)
    mixed = jax.nn.sigmoid(ga_ref[...]) * rw + jax.nn.sigmoid(gb_ref[...]) * at
    o_ref[...] = mixed.astype(o_ref.dtype)


def _mix(rw, at, w_rw, w_at, z, gate_col0, *, tm, tn):
    m, kr = rw.shape
    ka = at.shape[1]
    d = w_rw.shape[1]
    nb = d // tn
    g0 = gate_col0 // tn
    return pl.pallas_call(
        _mix_kernel,
        grid=(m // tm, nb),
        in_specs=[
            pl.BlockSpec((tm, kr), lambda i, j: (i, 0)),
            pl.BlockSpec((tm, ka), lambda i, j: (i, 0)),
            pl.BlockSpec((kr, tn), lambda i, j: (0, j)),
            pl.BlockSpec((ka, tn), lambda i, j: (0, j)),
            pl.BlockSpec((tm, tn), lambda i, j: (i, g0 + j)),
            pl.BlockSpec((tm, tn), lambda i, j: (i, g0 + nb + j)),
        ],
        out_specs=pl.BlockSpec((tm, tn), lambda i, j: (i, j)),
        out_shape=jax.ShapeDtypeStruct((m, d), BF16),
        compiler_params=_params(("parallel", "arbitrary"), 48),
        name="mix",
    )(rw, at, w_rw, w_at, z, z)


def _split_bf16(x):
    hi = x.astype(BF16)
    lo = (x - hi.astype(F32)).astype(BF16)
    return hi, lo


def _seg_sum_bcast(x, seg, seg_t):
    hi, lo = _split_bf16(x)
    s = jnp.dot(hi, seg, preferred_element_type=F32) + jnp.dot(lo, seg, preferred_element_type=F32)
    shi, slo = _split_bf16(s)
    return jnp.dot(shi, seg_t, preferred_element_type=F32) + jnp.dot(slo, seg_t, preferred_element_type=F32)


def _prep_body(cur, prev, mu, w0, wup, a0, aup, gup, k_k, k_a, r_k, seg, seg_t, outs):
    r_ref, w_ref, k_ref, v_ref, kkn_ref, b_ref, g_ref, bonus_ref = outs
    xs = {n: cur[n] + (prev[n] - cur[n]) * mu[n] for n in ("r", "k", "v", "t")}
    r, k, v, t = xs["r"], xs["k"], xs["v"], xs["t"]
    wl = w0 + jnp.dot(jnp.tanh(t).astype(BF16), wup, preferred_element_type=F32)
    w_log = -(jnp.maximum(-wl, 0.0) + jnp.log(1.0 + jnp.exp(-jnp.abs(wl)))) - 0.5
    decay = jnp.exp(-jnp.exp(w_log))
    a = jax.nn.sigmoid(a0 + jnp.dot(t.astype(BF16), aup, preferred_element_type=F32))
    g = jnp.dot(jax.nn.sigmoid(t).astype(BF16), gup, preferred_element_type=F32)
    kk = k * k_k
    kk = kk * lax.rsqrt(jnp.maximum(_seg_sum_bcast(kk * kk, seg, seg_t), 1e-24))
    k_mod = k * (1.0 + (a - 1.0) * k_a)
    bonus = _seg_sum_bcast(r * k_mod * r_k, seg, seg_t) * v
    r_ref[...] = r
    w_ref[...] = decay
    k_ref[...] = k_mod
    v_ref[...] = v
    kkn_ref[...] = -kk
    b_ref[...] = kk * a
    g_ref[...] = g
    bonus_ref[...] = bonus


def _prep_roll_kernel(zr, zk, zv, zt, br, bk, bv, bt, mr, mk, mv, mt, w0, wup, a0, aup, gup, k_k, k_a, r_k, seg,
                      seg_t, *outs):
    cur = {"r": zr[...], "k": zk[...], "v": zv[...], "t": zt[...]}
    prev = {}
    for n, bref in (("r", br), ("k", bk), ("v", bv), ("t", bt)):
        x = cur[n]
        rolled = pltpu.roll(x, 1, 0)
        row0 = lax.broadcasted_iota(jnp.int32, x.shape, 0) == 0
        prev[n] = jnp.where(row0, bref[0:1, :], rolled)
    mu = {"r": mr[...], "k": mk[...], "v": mv[...], "t": mt[...]}
    _prep_body(cur, prev, mu, w0[...], wup[...], a0[...], aup[...], gup[...], k_k[...], k_a[...], r_k[...],
               seg[...], seg_t[...], outs)


def _prep_given_kernel(zr, zk, zv, zt, pr, pk, pv, pt, mr, mk, mv, mt, w0, wup, a0, aup, gup, k_k, k_a, r_k, seg,
                       seg_t, *outs):
    cur = {"r": zr[...], "k": zk[...], "v": zv[...], "t": zt[...]}
    prev = {"r": pr[...], "k": pk[...], "v": pv[...], "t": pt[...]}
    mu = {"r": mr[...], "k": mk[...], "v": mv[...], "t": mt[...]}
    _prep_body(cur, prev, mu, w0[...], wup[...], a0[...], aup[...], gup[...], k_k[...], k_a[...], r_k[...],
               seg[...], seg_t[...], outs)


def _rwkv_prep(z, row0, rows, rw_col0, prev_arr, bound_arr, p, *, tm):
    c = p["w0"].shape[1]
    hc = c // 2
    nrb = rows // tm
    rb0 = row0 // tm
    cb = rw_col0 // hc
    tb = (rw_col0 + 3 * c) // 512
    zspec = lambda off: pl.BlockSpec((tm, hc), lambda i, j: (rb0 + i, cb + off + j))
    in_specs = [zspec(0), zspec(2), zspec(4), pl.BlockSpec((tm, 512), lambda i, j: (rb0 + i, tb))]
    args = [z, z, z, z]
    if prev_arr is not None:
        kern = _prep_given_kernel
        pspec = lambda off: pl.BlockSpec((tm, hc), lambda i, j: (i, off + j))
        in_specs += [pspec(0), pspec(2), pspec(4), pl.BlockSpec((tm, 512), lambda i, j: (i, 3 * c // 512))]
        args += [prev_arr] * 4
    else:
        kern = _prep_roll_kernel
        bspec = lambda off: pl.BlockSpec((None, 8, hc), lambda i, j: (i, 0, off + j))
        in_specs += [bspec(0), bspec(2), bspec(4), pl.BlockSpec((None, 8, 512), lambda i, j: (i, 0, 3 * c // 512))]
        args += [bound_arr] * 4
    mspec = lambda off: pl.BlockSpec((1, hc), lambda i, j: (0, off + j))
    in_specs += [mspec(0), mspec(2), mspec(4), pl.BlockSpec((1, 512), lambda i, j: (0, 3 * c // 512))]
    args += [p["mu"]] * 4
    vec = pl.BlockSpec((1, hc), lambda i, j: (0, j))
    lowrank = pl.BlockSpec((512, hc), lambda i, j: (0, j))
    in_specs += [vec, lowrank, vec, lowrank, lowrank, vec, vec, vec,
                 pl.BlockSpec((hc, LANES), lambda i, j: (0, 0)), pl.BlockSpec((LANES, hc), lambda i, j: (0, 0))]
    args += [p["w0"], p["wup"], p["a0"], p["aup"], p["gup"], p["k_k"], p["k_a"], p["r_k"], p["seg"], p["seg_t"]]
    out_spec = pl.BlockSpec((tm, hc), lambda i, j: (i, j))
    return pl.pallas_call(
        kern,
        grid=(nrb, 2),
        in_specs=in_specs,
        out_specs=[out_spec] * 8,
        out_shape=[jax.ShapeDtypeStruct((rows, c), F32)] * 8,
        compiler_params=_params(("parallel", "arbitrary"), 48),
        name="rwkv_prep",
    )(*args)


N_PAIRS = 16


def _trunc_bf16(x):
    return lax.bitcast_convert_type(lax.bitcast_convert_type(x, jnp.int32) & jnp.int32(-65536), F32)


def _wkv_kernel(r_ref, w_ref, k_ref, v_ref, kkn_ref, b_ref, s0_ref, y_ref, st_ref, s_scr, y_scr, *, tc):
    ci = pl.program_id(1)

    @pl.when(ci == 0)
    def _():
        for p in range(N_PAIRS):
            s_scr[p] = jnp.concatenate([s0_ref[2 * p], s0_ref[2 * p + 1]], axis=1)

    y_scr[...] = jnp.zeros(y_scr.shape, F32)
    lane = lax.broadcasted_iota(jnp.int32, (RW_HEAD, LANES), 1)
    sub = lax.broadcasted_iota(jnp.int32, (RW_HEAD, LANES), 0)
    diag = (lane & (RW_HEAD - 1)) == sub
    wr = lax.broadcasted_iota(jnp.int32, (2 * LANES, LANES), 0)
    wc = lax.broadcasted_iota(jnp.int32, (2 * LANES, LANES), 1)
    ones_bd = (((wr & (LANES - 1)) // RW_HEAD) == (wc // RW_HEAD)).astype(F32)
    zero_tile = jnp.zeros((RW_HEAD, LANES), F32)

    def step_group(gi, carry):
        t0 = pl.multiple_of(gi * SUBLANES, SUBLANES)
        tp0 = pl.multiple_of(jnp.maximum(gi - 1, 0) * SUBLANES, SUBLANES)
        rows8 = lambda ref, p: ref[pl.ds(t0, SUBLANES), LANES * p:LANES * (p + 1)]
        kkn8 = [rows8(kkn_ref, p) for p in range(N_PAIRS)]
        r8 = [rows8(r_ref, p) for p in range(N_PAIRS)]
        v8 = [rows8(v_ref, p) for p in range(N_PAIRS)]
        w8 = [rows8(w_ref, p) for p in range(N_PAIRS)]
        b8 = [rows8(b_ref, p) for p in range(N_PAIRS)]
        k8 = [rows8(k_ref, p) for p in range(N_PAIRS)]
        r_last = [r_ref[pl.ds(tp0, SUBLANES), LANES * p:LANES * (p + 1)][SUBLANES - 1:] for p in range(N_PAIRS)]
        for u in range(SUBLANES):
            ymask = (lane & (RW_HEAD - 1)) == (t0 + (u - 1))
            lhs = []
            for p in range(N_PAIRS):
                s_p = s_scr[p]
                pr = s_p * kkn8[p][u:u + 1]
                pr_hi = _trunc_bf16(pr)
                q = s_p * (r8[p][u - 1:u] if u else r_last[p])
                vrow = v8[p][u:u + 1]
                v_hi = _trunc_bf16(vrow)
                v_lo = vrow - v_hi
                lhs.append(jnp.concatenate([pr_hi, pr - pr_hi], axis=1))
                lhs.append(jnp.concatenate([q, zero_tile], axis=1))
                lhs.append(jnp.concatenate([jnp.where(diag, v_hi, 0.0), jnp.where(diag, v_lo, 0.0)], axis=1))
            res = jnp.dot(jnp.concatenate(lhs, axis=0), ones_bd, preferred_element_type=F32)
            for p in range(N_PAIRS):
                base = 3 * RW_HEAD * p
                sa = res[base:base + RW_HEAD]
                yb = res[base + RW_HEAD:base + 2 * RW_HEAD]
                vb = res[base + 2 * RW_HEAD:base + 3 * RW_HEAD]
                s_scr[p] = s_scr[p] * w8[p][u:u + 1] + sa * b8[p][u:u + 1] + vb * k8[p][u:u + 1]
                y_scr[p] = jnp.where(ymask, yb, y_scr[p])
        return carry

    lax.fori_loop(0, tc // SUBLANES, step_group, 0)

    ymask = (lane & (RW_HEAD - 1)) == (tc - 1)
    lhs = [s_scr[p] * r_ref[pl.ds(tc - 1, 1), slice(LANES * p, LANES * (p + 1))] for p in range(N_PAIRS)]
    res = jnp.dot(jnp.concatenate(lhs, axis=0), ones_bd[:LANES], preferred_element_type=F32)
    for p in range(N_PAIRS):
        ycols = jnp.where(ymask, res[RW_HEAD * p:RW_HEAD * (p + 1)], y_scr[p])
        yt = ycols.T
        rows = jnp.concatenate([yt[:RW_HEAD], yt[RW_HEAD:]], axis=1)
        y_ref[:, LANES * p:LANES * (p + 1)] = rows[:tc]

    @pl.when(ci == pl.num_programs(1) - 1)
    def _():
        for p in range(N_PAIRS):
            s_p = s_scr[p]
            st_ref[2 * p] = s_p[:, :RW_HEAD]
            st_ref[2 * p + 1] = s_p[:, RW_HEAD:]


def _wkv_scan(r, w, k, v, kkn, b, s0, *, nb, t, tc):
    c = r.shape[1]
    h = c // RW_HEAD
    nc = t // tc
    row = pl.BlockSpec((tc, c), lambda bi, ci: (bi * nc + ci, 0))
    st = pl.BlockSpec((None, h, RW_HEAD, RW_HEAD), lambda bi, ci: (bi, 0, 0, 0))
    return pl.pallas_call(
        functools.partial(_wkv_kernel, tc=tc),
        grid=(nb, nc),
        in_specs=[row] * 6 + [st],
        out_specs=[row, st],
        out_shape=[jax.ShapeDtypeStruct((nb * t, c), F32), jax.ShapeDtypeStruct((nb, h, RW_HEAD, RW_HEAD), F32)],
        scratch_shapes=[pltpu.VMEM((N_PAIRS, RW_HEAD, LANES), F32), pltpu.VMEM((N_PAIRS, RW_HEAD, LANES), F32)],
        compiler_params=_params(("parallel", "arbitrary"), 48),
        name="wkv_scan",
    )(r, w, k, v, kkn, b, s0)


def _rw_post_kernel(y_ref, g_ref, bonus_ref, lw_ref, lb_ref, seg_ref, segt_ref, o_ref):
    y = y_ref[...]
    seg = seg_ref[...]
    seg_t = segt_ref[...]
    inv = 1.0 / RW_HEAD
    mu = _seg_sum_bcast(y, seg, seg_t) * inv
    d = y - mu
    var = _seg_sum_bcast(d * d, seg, seg_t) * inv
    yn = d * lax.rsqrt(var + GN_EPS)
    o_ref[...] = ((yn * lw_ref[...] + lb_ref[...] + bonus_ref[...]) * g_ref[...]).astype(o_ref.dtype)


def _rw_post(y, g, bonus, lnx_w, lnx_b, seg, seg_t, *, tm):
    m, c = y.shape
    blk = pl.BlockSpec((tm, c), lambda i: (i, 0))
    vec = pl.BlockSpec((1, c), lambda i: (0, 0))
    return pl.pallas_call(
        _rw_post_kernel,
        grid=(m // tm,),
        in_specs=[blk, blk, blk, vec, vec, pl.BlockSpec((c, LANES), lambda i: (0, 0)),
                  pl.BlockSpec((LANES, c), lambda i: (0, 0))],
        out_specs=blk,
        out_shape=jax.ShapeDtypeStruct((m, c), BF16),
        compiler_params=_params(("parallel",), 48),
        name="rw_post",
    )(y, g, bonus, lnx_w, lnx_b, seg, seg_t)


QB = 128


def _attn_prompt_kernel(sl_ref, q0, k0, v0, q1, k1, v1, q2, k2, v2, out_ref, o_scr, l_scr, *, t):
    h = pl.program_id(1)
    scale = ATT_HEAD_DIM ** -0.5
    iq = lax.broadcasted_iota(jnp.int32, (QB, QB), 0)
    ik = lax.broadcasted_iota(jnp.int32, (QB, QB), 1)
    d_cur = (iq - ik).astype(F32)
    refs = ((q0, k0, v0), (q1, k1, v1), (q2, k2, v2))
    for g, (win, dil) in enumerate(ATT_GROUPS):
        q_ref, k_ref, v_ref = refs[g]
        slope = sl_ref[g * ATT_HEADS + h]
        nblk = t // dil // QB
        bias_cur = jnp.where(ik <= iq, -slope * dil * d_cur, NEG_INF)
        bias_prev = jnp.where(ik >= iq, -slope * dil * (d_cur + QB), NEG_INF)

        def block(idx, carry, q_ref=q_ref, k_ref=k_ref, v_ref=v_ref, dil=dil, nblk=nblk, g=g,
                  bias_cur=bias_cur, bias_prev=bias_prev):
            res = idx // nblk
            qb = idx % nblk
            base = res + qb * (QB * dil)
            pbase = res + jnp.maximum(qb - 1, 0) * (QB * dil)
            rows = pl.ds(base, QB, stride=dil) if dil > 1 else pl.ds(base, QB)
            prows = pl.ds(pbase, QB, stride=dil) if dil > 1 else pl.ds(pbase, QB)
            q = q_ref[rows, :].astype(BF16)
            kc = k_ref[rows, :].astype(BF16)
            kp = k_ref[prows, :].astype(BF16)
            dims = (((1,), (1,)), ((), ()))
            s_c = lax.dot_general(q, kc, dims, preferred_element_type=F32) * scale + bias_cur
            s_p = lax.dot_general(q, kp, dims, preferred_element_type=F32) * scale + jnp.where(
                qb > 0, bias_prev, NEG_INF)
            m = jnp.maximum(jnp.max(s_c, axis=-1, keepdims=True), jnp.max(s_p, axis=-1, keepdims=True))
            p_c = jnp.exp(s_c - m)
            p_p = jnp.exp(s_p - m)
            l = jnp.sum(p_c, axis=-1, keepdims=True) + jnp.sum(p_p, axis=-1, keepdims=True)
            o = (jnp.dot(p_c.astype(BF16), v_ref[rows, :].astype(BF16), preferred_element_type=F32)
                 + jnp.dot(p_p.astype(BF16), v_ref[prows, :].astype(BF16), preferred_element_type=F32)) / l
            o_scr[g, rows, :] = o
            l_scr[g, rows, :] = jnp.broadcast_to(m + jnp.log(l), (QB, ATT_HEAD_DIM))
            return carry

        lax.fori_loop(0, dil * nblk, block, 0)

    l0, l1, l2 = l_scr[0], l_scr[1], l_scr[2]
    m = jnp.maximum(jnp.maximum(l0, l1), l2)
    e0, e1, e2 = jnp.exp(l0 - m), jnp.exp(l1 - m), jnp.exp(l2 - m)
    den = e0 + e1 + e2
    out_ref[...] = ((e0 * o_scr[0] + e1 * o_scr[1] + e2 * o_scr[2]) / den).astype(out_ref.dtype)


def _attn_prompt(z, slopes, *, nb, t):
    specs = []
    for g in range(len(ATT_GROUPS)):
        for c in range(3):
            cb = (g * 3 + c) * ATT_HEADS
            specs.append(pl.BlockSpec((t, ATT_HEAD_DIM), lambda b, h, sl, cb=cb: (b, cb + h)))
    grid_spec = pltpu.PrefetchScalarGridSpec(
        num_scalar_prefetch=1,
        grid=(nb, ATT_HEADS),
        in_specs=specs,
        out_specs=pl.BlockSpec((t, ATT_HEAD_DIM), lambda b, h, sl: (b, h)),
        scratch_shapes=[pltpu.VMEM((3, t, ATT_HEAD_DIM), F32), pltpu.VMEM((3, t, ATT_HEAD_DIM), F32)],
    )
    return pl.pallas_call(
        functools.partial(_attn_prompt_kernel, t=t),
        grid_spec=grid_spec,
        out_shape=jax.ShapeDtypeStruct((nb * t, ATT_HEADS * ATT_HEAD_DIM), BF16),
        compiler_params=_params(("parallel", "arbitrary"), 48),
        name="attn_prompt",
    )(slopes, *([z] * 9))


N_NEW = 8
N_BAND = 128


def _attn_sample_kernel(q0, q1, q2, n0, n1, n2, c0, c1, c2, sl_ref, out_ref, sc_scr, o_scr, l_scr):
    scale = ATT_HEAD_DIM ** -0.5
    qs, news, caches = (q0, q1, q2), (n0, n1, n2), (c0, c1, c2)
    tile = (ATT_HEADS, ATT_HEAD_DIM)
    for g, (win, dil) in enumerate(ATT_GROUPS):
        q_ref, n_ref, c_ref = qs[g], news[g], caches[g]
        slope = sl_ref[g]
        past = N_BAND * dil
        qv = [q_ref[s] * scale for s in range(N_NEW)]
        if dil == 1:
            combos = [(0, s) for s in range(N_NEW)]
        elif dil == 4:
            combos = [(r, s) for r in range(4) for s in (r, r + 4)]
        else:
            combos = [(r, r) for r in range(N_NEW)]

        def row_tiles(m, r):
            if dil == 1:
                return c_ref[m, 0], c_ref[m, 1]
            if dil == 4:
                a = m * 4 + r
                return c_ref[a, 0], c_ref[a, 1]
            return c_ref[m, r, 0], c_ref[m, r, 1]

        def score(m, r, s):
            a = m * dil + r
            k_t, _ = row_tiles(m, r)
            sc = jnp.sum(qv[s] * k_t, axis=-1, keepdims=True) - slope * (past + s - a).astype(F32)
            if dil == 1:
                sc = jnp.where(m >= s, sc, NEG_INF)
            elif dil == 4 and s >= 4:
                sc = jnp.where(m >= 1, sc, NEG_INF)
            return jnp.broadcast_to(sc, tile)

        new_sc = []
        for s in range(N_NEW):
            lst = []
            for j in range(s // dil + 1):
                s2 = s - j * dil
                sc = jnp.sum(qv[s] * n_ref[s2, 0], axis=-1, keepdims=True) - slope * float(j * dil)
                lst.append((s2, jnp.broadcast_to(sc, tile)))
            new_sc.append(lst)
        m0 = [functools.reduce(jnp.maximum, [x for _, x in new_sc[s]]) for s in range(N_NEW)]

        def pass1(m, mx):
            mx = list(mx)
            for ci, (r, s) in enumerate(combos):
                sc = score(m, r, s)
                sc_scr[m, ci] = sc
                mx[s] = jnp.maximum(mx[s], sc)
            return tuple(mx)

        mx = lax.fori_loop(0, N_BAND, pass1, tuple(m0), unroll=8)

        l_init, o_init = [], []
        for s in range(N_NEW):
            l_s = jnp.zeros(tile, F32)
            o_s = jnp.zeros(tile, F32)
            for s2, sc in new_sc[s]:
                p = jnp.exp(sc - mx[s])
                l_s = l_s + p
                o_s = o_s + p * n_ref[s2, 1]
            l_init.append(l_s)
            o_init.append(o_s)

        def pass2(m, carry):
            ls, os_ = list(carry[0]), list(carry[1])
            for ci, (r, s) in enumerate(combos):
                _, v_t = row_tiles(m, r)
                p = jnp.exp(sc_scr[m, ci] - mx[s])
                ls[s] = ls[s] + p
                os_[s] = os_[s] + p * v_t
            return tuple(ls), tuple(os_)

        ls, os_ = lax.fori_loop(0, N_BAND, pass2, (tuple(l_init), tuple(o_init)), unroll=8)
        for s in range(N_NEW):
            o_scr[g, s] = os_[s] / ls[s]
            l_scr[g, s] = mx[s] + jnp.log(ls[s])

    for s in range(N_NEW):
        l0, l1, l2 = l_scr[0, s], l_scr[1, s], l_scr[2, s]
        m = jnp.maximum(jnp.maximum(l0, l1), l2)
        e0, e1, e2 = jnp.exp(l0 - m), jnp.exp(l1 - m), jnp.exp(l2 - m)
        out_ref[s] = (e0 * o_scr[0, s] + e1 * o_scr[1, s] + e2 * o_scr[2, s]) / (e0 + e1 + e2)


def _attn_sample(qs, news, caches, slope_tiles):
    nb = qs[0].shape[0]
    he = (ATT_HEADS, ATT_HEAD_DIM)
    qspec = pl.BlockSpec((None, N_NEW) + he, lambda b: (b, 0, 0, 0))
    nspec = pl.BlockSpec((None, N_NEW, 2) + he, lambda b: (b, 0, 0, 0, 0))
    c2 = caches[2]
    dil2 = ATT_GROUPS[2][1]
    c2v = c2.reshape(nb, c2.shape[1] // dil2, dil2, 2, *he)
    cspecs = [
        pl.BlockSpec((None, caches[0].shape[1], 2) + he, lambda b: (b, 0, 0, 0, 0)),
        pl.BlockSpec((None, caches[1].shape[1], 2) + he, lambda b: (b, 0, 0, 0, 0)),
        pl.BlockSpec((None, N_BAND, N_NEW, 2) + he, lambda b: (b, 0, 0, 0, 0, 0)),
    ]
    return pl.pallas_call(
        _attn_sample_kernel,
        grid=(nb,),
        in_specs=[qspec] * 3 + [nspec] * 3 + cspecs + [pl.BlockSpec((3,) + he, lambda b: (0, 0, 0))],
        out_specs=pl.BlockSpec((None, N_NEW) + he, lambda b: (b, 0, 0, 0)),
        out_shape=jax.ShapeDtypeStruct((nb, N_NEW) + he, F32),
        scratch_shapes=[pltpu.VMEM((N_BAND, N_NEW) + he, F32), pltpu.VMEM((3, N_NEW) + he, F32),
                        pltpu.VMEM((3, N_NEW) + he, F32)],
        compiler_params=_params(("parallel",), 48),
        name="attn_sample",
    )(*qs, *news, caches[0], caches[1], c2v, slope_tiles)


def kernel(x_prompt, x_sample, cache_kv_0, cache_kv_1, cache_kv_2, state_wkv, state_shift, norm_mix, w_in, rw_mu,
           rw_w0, rw_w_up, rw_a0, rw_a_up, rw_g_up, rw_k_k, rw_k_a, rw_r_k, rw_lnx_w, rw_lnx_b, rw_out, att_out,
           w_o, norm_ffn, w_ffn_in, w_ffn_out, norm_final):
    depth = w_in.shape[0]
    assert depth == 1, "single-layer stack"
    bp, tp, d = x_prompt.shape
    bs, ts, _ = x_sample.shape
    mp, ms = bp * tp, bs * ts
    assert ts == N_NEW
    c = rw_w0.shape[1]
    r_dec, r_aaa, r_gate = rw_w_up.shape[1], rw_a_up.shape[1], rw_g_up.shape[1]
    rw_cols = 3 * c + r_dec + r_aaa + r_gate
    att_cols = len(ATT_GROUPS) * 3 * ATT_HEADS * ATT_HEAD_DIM
    att_w = ATT_HEADS * ATT_HEAD_DIM
    pad = (-(att_cols + rw_cols)) % LANES
    rwp = rw_cols + pad
    assert rwp - 3 * c == 512
    gate0 = att_cols + rwp

    wi = w_in[0]
    w_in_b = jnp.concatenate(
        [wi[:, :att_cols + rw_cols].astype(BF16), jnp.zeros((d, pad), BF16), wi[:, att_cols + rw_cols:].astype(BF16)],
        axis=1)
    tail = lambda w, off: jnp.zeros((512, c), F32).at[off:off + w.shape[0]].set(w).astype(BF16)
    row = lambda v, n: jnp.pad(v.reshape(1, -1), ((0, 0), (0, n - v.size)))
    seg = (jnp.arange(c // 2)[:, None] // RW_HEAD == jnp.arange(LANES)[None, :]).astype(BF16)
    prm = {
        "mu": row(rw_mu[0], rwp), "w0": row(rw_w0[0], c), "a0": row(rw_a0[0], c),
        "wup": tail(rw_w_up[0], 0), "aup": tail(rw_a_up[0], r_dec), "gup": tail(rw_g_up[0], r_dec + r_aaa),
        "k_k": row(rw_k_k[0], c), "k_a": row(rw_k_a[0], c), "r_k": row(rw_r_k[0].reshape(-1), c),
        "seg": seg, "seg_t": seg.T,
    }
    seg_full = (jnp.arange(c)[:, None] // RW_HEAD == jnp.arange(LANES)[None, :]).astype(BF16)

    x = jnp.concatenate([x_prompt.reshape(mp, d), x_sample.reshape(ms, d)], axis=0)
    m_all = mp + ms
    hmix = _rmsnorm(x, norm_mix[0], BF16)
    z = _matmul(hmix, w_in_b, tm=1024, tn=512, name="w_in")
    z_p = z[:mp].reshape(bp, tp, -1)
    z_s = z[mp:].reshape(bs, ts, -1)

    new_kv_p, new_kv_s = [], []
    for g, (win, dil) in enumerate(ATT_GROUPS):
        lo = g * 3 * att_w + att_w
        keep = min(win, tp)
        new_kv_p.append(z_p[:, tp - keep:, lo:lo + 2 * att_w].reshape(bp, keep, 2, ATT_HEADS, ATT_HEAD_DIM)[None])
        new_kv_s.append(z_s[:, :, lo:lo + 2 * att_w].reshape(bs, ts, 2, ATT_HEADS, ATT_HEAD_DIM)[None])
    shift_p = z_p[:, -1, att_cols:att_cols + rw_cols][None]
    shift_s = z_s[:, -1, att_cols:att_cols + rw_cols][None]

    tm_prep = 256
    last_rows = z[tm_prep - 1:mp:tm_prep, att_cols:att_cols + rwp]
    bound0 = jnp.concatenate([jnp.zeros((1, rwp), F32), last_rows[:-1]], axis=0)
    tiles_per_seq = tp // tm_prep
    seq_start = (jnp.arange(mp // tm_prep) % tiles_per_seq == 0)[:, None]
    bound0 = jnp.where(seq_start, 0.0, bound0)
    bound = jnp.pad(bound0[:, None, :], ((0, 0), (0, 7), (0, 0)))
    prep_p = _rwkv_prep(z, 0, mp, att_cols, None, bound, prm, tm=tm_prep)
    cols_s = z_s[:, :, att_cols:att_cols + rwp]
    shift_in = jnp.pad(state_shift[0], ((0, 0), (0, pad)))
    prev_s = jnp.concatenate([shift_in[:, None, :], cols_s[:, :-1]], axis=1).reshape(ms, rwp)
    prep_s = _rwkv_prep(z, mp, ms, att_cols, prev_s, None, prm, tm=tm_prep)

    h_rw = c // RW_HEAD
    y_p, wkv_p = _wkv_scan(*prep_p[:6], jnp.zeros((bp, h_rw, RW_HEAD, RW_HEAD), F32), nb=bp, t=tp, tc=64)
    y_s, wkv_s = _wkv_scan(*prep_s[:6], state_wkv[0], nb=bs, t=ts, tc=ts)
    lw, lb = rw_lnx_w[0].reshape(1, c), rw_lnx_b[0].reshape(1, c)
    rw_p = _rw_post(y_p, prep_p[6], prep_p[7], lw, lb, seg_full, seg_full.T, tm=256)
    rw_s = _rw_post(y_s, prep_s[6], prep_s[7], lw, lb, seg_full, seg_full.T, tm=256)
    rw_act = jnp.concatenate([rw_p, rw_s], axis=0)

    n_heads_all = len(ATT_GROUPS) * ATT_HEADS
    slopes = 2.0 ** (-8.0 * jnp.arange(1, n_heads_all + 1, dtype=F32) / n_heads_all)
    att_p = _attn_prompt(z, slopes, nb=bp, t=tp)
    qs = [z_s[:, :, g * 3 * att_w:g * 3 * att_w + att_w].reshape(bs, ts, ATT_HEADS, ATT_HEAD_DIM)
          for g in range(len(ATT_GROUPS))]
    slope_tiles = jnp.broadcast_to(slopes.reshape(len(ATT_GROUPS), ATT_HEADS, 1),
                                   (len(ATT_GROUPS), ATT_HEADS, ATT_HEAD_DIM))
    att_s = _attn_sample(qs, [kv[0] for kv in new_kv_s], [cache_kv_0[0], cache_kv_1[0], cache_kv_2[0]], slope_tiles)
    att_act = jnp.concatenate([att_p, att_s.reshape(ms, att_w).astype(BF16)], axis=0)

    mixed = _mix(rw_act, att_act, rw_out[0].astype(BF16), att_out[0].astype(BF16), z, gate0, tm=1024, tn=512)
    x1 = _matmul(mixed, w_o[0].astype(BF16), tm=1024, tn=512, residual=x, name="w_o")
    hf = _rmsnorm(x1, norm_ffn[0], BF16)
    act = _swiglu(hf, w_ffn_in[0].astype(BF16), tm=1024, tn=256)
    x2 = _matmul(act, w_ffn_out[0].astype(BF16), tm=512, tn=256, residual=x1, vmem_mib=56, name="w_ffn_out")
    y = _rmsnorm(x2, norm_final, F32)

    return (y[:mp].reshape(bp, tp, d), y[mp:].reshape(bs, ts, d),
            new_kv_p[0], new_kv_p[1], new_kv_p[2], wkv_p[None], shift_p,
            new_kv_s[0], new_kv_s[1], new_kv_s[2], wkv_s[None], shift_s)
```

```python
import functools

import jax
import jax.numpy as jnp
from jax import lax
from jax.experimental import pallas as pl
from jax.experimental.pallas import tpu as pltpu

F32 = jnp.float32
BF16 = jnp.bfloat16

RMS_EPS = 1e-6
GN_EPS = 64e-5
RW_HEAD = 64
ATT_GROUPS = ((128, 1), (512, 4), (2048, 16))
ATT_HEAD_DIM = 128
ATT_HEADS = 8
LANES = 128
SUBLANES = 8
NEG_INF = float("-inf")


def _params(sem, vmem_mib):
    return pltpu.CompilerParams(dimension_semantics=sem, vmem_limit_bytes=vmem_mib << 20)


def _rmsnorm_kernel(x_ref, g_ref, o_ref):
    x = x_ref[...]
    ms = jnp.mean(x * x, axis=-1, keepdims=True)
    o_ref[...] = (x * lax.rsqrt(ms + RMS_EPS) * g_ref[...]).astype(o_ref.dtype)


def _rmsnorm(x, g, out_dtype, row0=0, rows=None, tm=256):
    d = x.shape[1]
    rows = x.shape[0] if rows is None else rows
    rb0 = row0 // tm
    return pl.pallas_call(
        _rmsnorm_kernel,
        grid=(rows // tm,),
        in_specs=[pl.BlockSpec((tm, d), lambda i: (rb0 + i, 0)), pl.BlockSpec((1, d), lambda i: (0, 0))],
        out_specs=pl.BlockSpec((tm, d), lambda i: (i, 0)),
        out_shape=jax.ShapeDtypeStruct((rows, d), out_dtype),
        compiler_params=_params(("parallel",), 32),
        name="rmsnorm",
    )(x, g.reshape(1, d))


def _mm_kernel(a_ref, w_ref, o_ref):
    o_ref[...] = jnp.dot(a_ref[...], w_ref[...].astype(BF16), preferred_element_type=F32).astype(o_ref.dtype)


def _mm_res_kernel(a_ref, w_ref, r_ref, o_ref):
    o_ref[...] = r_ref[...] + jnp.dot(a_ref[...], w_ref[...].astype(BF16), preferred_element_type=F32)


def _matmul(a, w, *, tm, tn, n_out=None, residual=None, vmem_mib=48, name="matmul"):
    m, k = a.shape
    n = w.shape[1] if n_out is None else n_out
    in_specs = [pl.BlockSpec((tm, k), lambda i, j: (i, 0)), pl.BlockSpec((k, tn), lambda i, j: (0, j))]
    args = [a, w]
    kern = _mm_kernel
    if residual is not None:
        in_specs.append(pl.BlockSpec((tm, tn), lambda i, j: (i, j)))
        args.append(residual)
        kern = _mm_res_kernel
    return pl.pallas_call(
        kern,
        grid=(m // tm, n // tn),
        in_specs=in_specs,
        out_specs=pl.BlockSpec((tm, tn), lambda i, j: (i, j)),
        out_shape=jax.ShapeDtypeStruct((m, n), F32),
        compiler_params=_params(("parallel", "arbitrary"), vmem_mib),
        name=name,
    )(*args)


def _swiglu_kernel(a_ref, wg_ref, wu_ref, o_ref):
    a = a_ref[...]
    g = jnp.dot(a, wg_ref[...].astype(BF16), preferred_element_type=F32)
    u = jnp.dot(a, wu_ref[...].astype(BF16), preferred_element_type=F32)
    o_ref[...] = (g * jax.nn.sigmoid(g) * u).astype(o_ref.dtype)


def _swiglu(a, w, *, tm, tn):
    m, k = a.shape
    f = w.shape[1] // 2
    nb = f // tn
    return pl.pallas_call(
        _swiglu_kernel,
        grid=(m // tm, nb),
        in_specs=[
            pl.BlockSpec((tm, k), lambda i, j: (i, 0)),
            pl.BlockSpec((k, tn), lambda i, j: (0, j)),
            pl.BlockSpec((k, tn), lambda i, j: (0, j + nb)),
        ],
        out_specs=pl.BlockSpec((tm, tn), lambda i, j: (i, j)),
        out_shape=jax.ShapeDtypeStruct((m, f), BF16),
        compiler_params=_params(("parallel", "arbitrary"), 56),
        name="swiglu",
    )(a, w, w)


def _mix_kernel(rw_ref, at_ref, wr_ref, wa_ref, ga_ref, gb_ref, o_ref):
    rw = jnp.dot(rw_ref[...], wr_ref[...], preferred_element_type=F32)
    at = jnp.dot(at_ref[...], wa_ref[...], preferred_element_type=F32)
    mixed = jax.nn.sigmoid(ga_ref[...]) * rw + jax.nn.sigmoid(gb_ref[...]) * at
    o_ref[...] = mixed.astype(o_ref.dtype)


def _mix(rw, at, w_rw, w_at, z, gate_col0, *, tm, tn):
    m, kr = rw.shape
    ka = at.shape[1]
    d = w_rw.shape[1]
    nb = d // tn
    g0 = gate_col0 // tn
    return pl.pallas_call(
        _mix_kernel,
        grid=(m // tm, nb),
        in_specs=[
            pl.BlockSpec((tm, kr), lambda i, j: (i, 0)),
            pl.BlockSpec((tm, ka), lambda i, j: (i, 0)),
            pl.BlockSpec((kr, tn), lambda i, j: (0, j)),
            pl.BlockSpec((ka, tn), lambda i, j: (0, j)),
            pl.BlockSpec((tm, tn), lambda i, j: (i, g0 + j)),
            pl.BlockSpec((tm, tn), lambda i, j: (i, g0 + nb + j)),
        ],
        out_specs=pl.BlockSpec((tm, tn), lambda i, j: (i, j)),
        out_shape=jax.ShapeDtypeStruct((m, d), BF16),
        compiler_params=_params(("parallel", "arbitrary"), 48),
        name="mix",
    )(rw, at, w_rw, w_at, z, z)


def _split_bf16(x):
    hi = x.astype(BF16)
    lo = (x - hi.astype(F32)).astype(BF16)
    return hi, lo


def _seg_sum_bcast(x, seg, seg_t):
    hi, lo = _split_bf16(x)
    s = jnp.dot(hi, seg, preferred_element_type=F32) + jnp.dot(lo, seg, preferred_element_type=F32)
    shi, slo = _split_bf16(s)
    return jnp.dot(shi, seg_t, preferred_element_type=F32) + jnp.dot(slo, seg_t, preferred_element_type=F32)


def _prep_body(cur, prev, mu, w0, wup, a0, aup, gup, k_k, k_a, r_k, seg, seg_t, outs):
    r_ref, w_ref, k_ref, v_ref, kkn_ref, b_ref, g_ref, bonus_ref = outs
    xs = {n: cur[n] + (prev[n] - cur[n]) * mu[n] for n in ("r", "k", "v", "t")}
    r, k, v, t = xs["r"], xs["k"], xs["v"], xs["t"]
    wl = w0 + jnp.dot(jnp.tanh(t).astype(BF16), wup, preferred_element_type=F32)
    w_log = -(jnp.maximum(-wl, 0.0) + jnp.log(1.0 + jnp.exp(-jnp.abs(wl)))) - 0.5
    decay = jnp.exp(-jnp.exp(w_log))
    a = jax.nn.sigmoid(a0 + jnp.dot(t.astype(BF16), aup, preferred_element_type=F32))
    g = jnp.dot(jax.nn.sigmoid(t).astype(BF16), gup, preferred_element_type=F32)
    kk = k * k_k
    kk = kk * lax.rsqrt(jnp.maximum(_seg_sum_bcast(kk * kk, seg, seg_t), 1e-24))
    k_mod = k * (1.0 + (a - 1.0) * k_a)
    bonus = _seg_sum_bcast(r * k_mod * r_k, seg, seg_t) * v
    r_ref[...] = r
    w_ref[...] = decay
    k_ref[...] = k_mod
    v_ref[...] = v
    kkn_ref[...] = -kk
    b_ref[...] = kk * a
    g_ref[...] = g
    bonus_ref[...] = bonus


def _prep_roll_kernel(zr, zk, zv, zt, br, bk, bv, bt, sr, sk, sv, st, mr, mk, mv, mt, w0, wup, a0, aup, gup, k_k,
                      k_a, r_k, seg, seg_t, *outs, tiles_per_seq):
    cur = {"r": zr[...], "k": zk[...], "v": zv[...], "t": zt[...]}
    seq_start = pl.program_id(0) % tiles_per_seq == 0
    prev = {}
    for n, bref, sref in (("r", br, sr), ("k", bk, sk), ("v", bv, sv), ("t", bt, st)):
        x = cur[n]
        rolled = pltpu.roll(x, 1, 0)
        row0 = lax.broadcasted_iota(jnp.int32, x.shape, 0) == 0
        first = jnp.where(seq_start, sref[...], bref[SUBLANES - 1:, :])
        prev[n] = jnp.where(row0, first, rolled)
    mu = {"r": mr[...], "k": mk[...], "v": mv[...], "t": mt[...]}
    _prep_body(cur, prev, mu, w0[...], wup[...], a0[...], aup[...], gup[...], k_k[...], k_a[...], r_k[...],
               seg[...], seg_t[...], outs)


def _prep_given_kernel(zr, zk, zv, zt, pr, pk, pv, pt, mr, mk, mv, mt, w0, wup, a0, aup, gup, k_k, k_a, r_k, seg,
                       seg_t, *outs):
    cur = {"r": zr[...], "k": zk[...], "v": zv[...], "t": zt[...]}
    prev = {"r": pr[...], "k": pk[...], "v": pv[...], "t": pt[...]}
    mu = {"r": mr[...], "k": mk[...], "v": mv[...], "t": mt[...]}
    _prep_body(cur, prev, mu, w0[...], wup[...], a0[...], aup[...], gup[...], k_k[...], k_a[...], r_k[...],
               seg[...], seg_t[...], outs)


def _rwkv_prep(z, row0, rows, rw_col0, prev_arr, shift_arr, seq_len, p, *, tm):
    c = p["w0"].shape[1]
    hc = c // 2
    nrb = rows // tm
    rb0 = row0 // tm
    cb = rw_col0 // hc
    tb = (rw_col0 + 3 * c) // 512
    zspec = lambda off: pl.BlockSpec((tm, hc), lambda i, j: (rb0 + i, cb + off + j))
    in_specs = [zspec(0), zspec(2), zspec(4), pl.BlockSpec((tm, 512), lambda i, j: (rb0 + i, tb))]
    args = [z, z, z, z]
    if prev_arr is not None:
        kern = _prep_given_kernel
        pspec = lambda off: pl.BlockSpec((tm, hc), lambda i, j: (i, off + j))
        in_specs += [pspec(0), pspec(2), pspec(4), pl.BlockSpec((tm, 512), lambda i, j: (i, 3 * c // 512))]
        args += [prev_arr] * 4
    else:
        tiles_per_seq = seq_len // tm
        kern = functools.partial(_prep_roll_kernel, tiles_per_seq=tiles_per_seq)
        before = lambda i: jnp.maximum((rb0 + i) * (tm // SUBLANES) - 1, 0)
        bspec = lambda off: pl.BlockSpec((SUBLANES, hc), lambda i, j: (before(i), cb + off + j))
        in_specs += [bspec(0), bspec(2), bspec(4), pl.BlockSpec((SUBLANES, 512), lambda i, j: (before(i), tb))]
        args += [z] * 4
        sspec = lambda off: pl.BlockSpec((None, 1, hc), lambda i, j: (i // tiles_per_seq, 0, off + j))
        in_specs += [sspec(0), sspec(2), sspec(4),
                     pl.BlockSpec((None, 1, 512), lambda i, j: (i // tiles_per_seq, 0, 3 * c // 512))]
        args += [shift_arr] * 4
    mspec = lambda off: pl.BlockSpec((1, hc), lambda i, j: (0, off + j))
    in_specs += [mspec(0), mspec(2), mspec(4), pl.BlockSpec((1, 512), lambda i, j: (0, 3 * c // 512))]
    args += [p["mu"]] * 4
    vec = pl.BlockSpec((1, hc), lambda i, j: (0, j))
    lowrank = pl.BlockSpec((512, hc), lambda i, j: (0, j))
    in_specs += [vec, lowrank, vec, lowrank, lowrank, vec, vec, vec,
                 pl.BlockSpec((hc, LANES), lambda i, j: (0, 0)), pl.BlockSpec((LANES, hc), lambda i, j: (0, 0))]
    args += [p["w0"], p["wup"], p["a0"], p["aup"], p["gup"], p["k_k"], p["k_a"], p["r_k"], p["seg"], p["seg_t"]]
    out_spec = pl.BlockSpec((tm, hc), lambda i, j: (i, j))
    return pl.pallas_call(
        kern,
        grid=(nrb, 2),
        in_specs=in_specs,
        out_specs=[out_spec] * 8,
        out_shape=[jax.ShapeDtypeStruct((rows, c), F32)] * 8,
        compiler_params=_params(("parallel", "arbitrary"), 48),
        name="rwkv_prep",
    )(*args)


N_PAIRS = 16
N_V_MXU = 0


def _trunc_bf16(x):
    return lax.bitcast_convert_type(lax.bitcast_convert_type(x, jnp.int32) & jnp.int32(-65536), F32)


def _wkv_kernel(r_ref, w_ref, k_ref, v_ref, kkn_ref, b_ref, s0_ref, y_ref, st_ref, s_scr, y_scr, *, tc):
    ci = pl.program_id(1)

    @pl.when(ci == 0)
    def _():
        for p in range(N_PAIRS):
            s_scr[p] = jnp.concatenate([s0_ref[2 * p], s0_ref[2 * p + 1]], axis=1)

    y_scr[...] = jnp.zeros(y_scr.shape, F32)
    lane = lax.broadcasted_iota(jnp.int32, (RW_HEAD, LANES), 1)
    sub = lax.broadcasted_iota(jnp.int32, (RW_HEAD, LANES), 0)
    head_a = lane < RW_HEAD
    diag_a = lane == sub
    diag_b = lane == sub + RW_HEAD
    diag = (lane & (RW_HEAD - 1)) == sub
    n_v = N_V_MXU * RW_HEAD
    wr = lax.broadcasted_iota(jnp.int32, (2 * LANES, 2 * LANES), 0)
    wc = lax.broadcasted_iota(jnp.int32, (2 * LANES, 2 * LANES), 1)
    ones_bd = ((wr // RW_HEAD) == (wc // RW_HEAD)).astype(F32)
    n_sa = N_PAIRS * RW_HEAD

    def readout(res, ymask):
        for p in range(N_PAIRS):
            blk = n_sa + RW_HEAD * (p // 2)
            yb = res[blk:blk + RW_HEAD, LANES * (p % 2):LANES * (p % 2 + 1)]
            y_scr[p] = jnp.where(ymask, yb, y_scr[p])

    def step_group(gi, carry):
        t0 = pl.multiple_of(gi * SUBLANES, SUBLANES)
        tp0 = pl.multiple_of(jnp.maximum(gi - 1, 0) * SUBLANES, SUBLANES)
        rows8 = lambda ref, p: ref[pl.ds(t0, SUBLANES), LANES * p:LANES * (p + 1)]
        kkn8 = [rows8(kkn_ref, p) for p in range(N_PAIRS)]
        r8 = [rows8(r_ref, p) for p in range(N_PAIRS)]
        v8 = [rows8(v_ref, p) for p in range(N_PAIRS)]
        w8 = [rows8(w_ref, p) for p in range(N_PAIRS)]
        b8 = [rows8(b_ref, p) for p in range(N_PAIRS)]
        k8 = [rows8(k_ref, p) for p in range(N_PAIRS)]
        r_last = [r_ref[pl.ds(tp0, SUBLANES), LANES * p:LANES * (p + 1)][SUBLANES - 1:] for p in range(N_PAIRS)]
        for u in range(SUBLANES):
            ymask = (lane & (RW_HEAD - 1)) == (t0 + (u - 1))
            lhs_sa, lhs_y, lhs_v, qs = [], [], [], []
            for p in range(N_PAIRS):
                s_p = s_scr[p]
                pr = s_p * kkn8[p][u:u + 1]
                pr_hi = _trunc_bf16(pr)
                lhs_sa.append(jnp.concatenate([pr_hi, pr - pr_hi], axis=1))
                qs.append(s_p * (r8[p][u - 1:u] if u else r_last[p]))
                if p % 2:
                    lhs_y.append(jnp.concatenate(qs[-2:], axis=1))
                if p < N_V_MXU:
                    vrow = v8[p][u:u + 1]
                    v_hi = _trunc_bf16(vrow)
                    lhs_v.append(jnp.concatenate(
                        [jnp.where(diag, v_hi, 0.0), jnp.where(diag, vrow - v_hi, 0.0)], axis=1))
            res = jnp.dot(jnp.concatenate(lhs_v + lhs_sa + lhs_y, axis=0), ones_bd, preferred_element_type=F32)
            res_v, res = res[:n_v], res[n_v:]
            for p in range(N_PAIRS):
                sa2 = res[RW_HEAD * p:RW_HEAD * (p + 1)]
                sa = sa2[:, :LANES] + sa2[:, LANES:]
                if p < N_V_MXU:
                    v2 = res_v[RW_HEAD * p:RW_HEAD * (p + 1)]
                    vcol = v2[:, :LANES] + v2[:, LANES:]
                else:
                    vrow = v8[p][u:u + 1]
                    va = jnp.sum(jnp.where(diag_a, vrow, 0.0), axis=1, keepdims=True)
                    vb = jnp.sum(jnp.where(diag_b, vrow, 0.0), axis=1, keepdims=True)
                    vcol = jnp.where(head_a, va, vb)
                s_scr[p] = s_scr[p] * w8[p][u:u + 1] + sa * b8[p][u:u + 1] + vcol * k8[p][u:u + 1]
            readout(res, ymask)
        return carry

    lax.fori_loop(0, tc // SUBLANES, step_group, 0)

    qs = [s_scr[p] * r_ref[pl.ds(tc - 1, 1), slice(LANES * p, LANES * (p + 1))] for p in range(N_PAIRS)]
    lhs_y = [jnp.concatenate(qs[p:p + 2], axis=1) for p in range(0, N_PAIRS, 2)]
    res = jnp.dot(jnp.concatenate(lhs_y, axis=0), ones_bd, preferred_element_type=F32)
    readout(jnp.concatenate([jnp.zeros((n_sa, 2 * LANES), F32), res], axis=0), (lane & (RW_HEAD - 1)) == (tc - 1))
    for p in range(N_PAIRS):
        yt = y_scr[p].T
        rows = jnp.concatenate([yt[:RW_HEAD], yt[RW_HEAD:]], axis=1)
        y_ref[:, LANES * p:LANES * (p + 1)] = rows[:tc]

    @pl.when(ci == pl.num_programs(1) - 1)
    def _():
        for p in range(N_PAIRS):
            s_p = s_scr[p]
            st_ref[2 * p] = s_p[:, :RW_HEAD]
            st_ref[2 * p + 1] = s_p[:, RW_HEAD:]


def _wkv_scan(r, w, k, v, kkn, b, s0, *, nb, t, tc):
    c = r.shape[1]
    h = c // RW_HEAD
    nc = t // tc
    row = pl.BlockSpec((tc, c), lambda bi, ci: (bi * nc + ci, 0))
    st = pl.BlockSpec((None, h, RW_HEAD, RW_HEAD), lambda bi, ci: (bi, 0, 0, 0))
    return pl.pallas_call(
        functools.partial(_wkv_kernel, tc=tc),
        grid=(nb, nc),
        in_specs=[row] * 6 + [st],
        out_specs=[row, st],
        out_shape=[jax.ShapeDtypeStruct((nb * t, c), F32), jax.ShapeDtypeStruct((nb, h, RW_HEAD, RW_HEAD), F32)],
        scratch_shapes=[pltpu.VMEM((N_PAIRS, RW_HEAD, LANES), F32), pltpu.VMEM((N_PAIRS, RW_HEAD, LANES), F32)],
        compiler_params=_params(("parallel", "arbitrary"), 48),
        name="wkv_scan",
    )(r, w, k, v, kkn, b, s0)


def _rw_post_kernel(y_ref, g_ref, bonus_ref, lw_ref, lb_ref, seg_ref, segt_ref, o_ref):
    y = y_ref[...]
    seg = seg_ref[...]
    seg_t = segt_ref[...]
    inv = 1.0 / RW_HEAD
    mu = _seg_sum_bcast(y, seg, seg_t) * inv
    d = y - mu
    var = _seg_sum_bcast(d * d, seg, seg_t) * inv
    yn = d * lax.rsqrt(var + GN_EPS)
    o_ref[...] = ((yn * lw_ref[...] + lb_ref[...] + bonus_ref[...]) * g_ref[...]).astype(o_ref.dtype)


def _rw_post(y, g, bonus, lnx_w, lnx_b, seg, seg_t, *, tm):
    m, c = y.shape
    blk = pl.BlockSpec((tm, c), lambda i: (i, 0))
    vec = pl.BlockSpec((1, c), lambda i: (0, 0))
    return pl.pallas_call(
        _rw_post_kernel,
        grid=(m // tm,),
        in_specs=[blk, blk, blk, vec, vec, pl.BlockSpec((c, LANES), lambda i: (0, 0)),
                  pl.BlockSpec((LANES, c), lambda i: (0, 0))],
        out_specs=blk,
        out_shape=jax.ShapeDtypeStruct((m, c), BF16),
        compiler_params=_params(("parallel",), 48),
        name="rw_post",
    )(y, g, bonus, lnx_w, lnx_b, seg, seg_t)


QB = 128


def _attn_prompt_kernel(sl_ref, q0, k0, v0, q1, k1, v1, q2, k2, v2, out_ref, o_scr, l_scr, *, t):
    h = pl.program_id(1)
    scale = ATT_HEAD_DIM ** -0.5
    iq = lax.broadcasted_iota(jnp.int32, (QB, QB), 0)
    ik = lax.broadcasted_iota(jnp.int32, (QB, QB), 1)
    d_cur = (iq - ik).astype(F32)
    refs = ((q0, k0, v0), (q1, k1, v1), (q2, k2, v2))
    for g, (win, dil) in enumerate(ATT_GROUPS):
        q_ref, k_ref, v_ref = refs[g]
        slope = sl_ref[g * ATT_HEADS + h]
        nblk = t // dil // QB
        bias_cur = jnp.where(ik <= iq, -slope * dil * d_cur, NEG_INF)
        bias_prev = jnp.where(ik >= iq, -slope * dil * (d_cur + QB), NEG_INF)

        def block(idx, carry, q_ref=q_ref, k_ref=k_ref, v_ref=v_ref, dil=dil, nblk=nblk, g=g,
                  bias_cur=bias_cur, bias_prev=bias_prev):
            res = idx // nblk
            qb = idx % nblk
            base = res + qb * (QB * dil)
            pbase = res + jnp.maximum(qb - 1, 0) * (QB * dil)
            rows = pl.ds(base, QB, stride=dil) if dil > 1 else pl.ds(base, QB)
            prows = pl.ds(pbase, QB, stride=dil) if dil > 1 else pl.ds(pbase, QB)
            q = q_ref[rows, :].astype(BF16)
            kc = k_ref[rows, :].astype(BF16)
            kp = k_ref[prows, :].astype(BF16)
            dims = (((1,), (1,)), ((), ()))
            s_c = lax.dot_general(q, kc, dims, preferred_element_type=F32) * scale + bias_cur
            s_p = lax.dot_general(q, kp, dims, preferred_element_type=F32) * scale + jnp.where(
                qb > 0, bias_prev, NEG_INF)
            m = jnp.maximum(jnp.max(s_c, axis=-1, keepdims=True), jnp.max(s_p, axis=-1, keepdims=True))
            p_c = jnp.exp(s_c - m)
            p_p = jnp.exp(s_p - m)
            l = jnp.sum(p_c, axis=-1, keepdims=True) + jnp.sum(p_p, axis=-1, keepdims=True)
            o = (jnp.dot(p_c.astype(BF16), v_ref[rows, :].astype(BF16), preferred_element_type=F32)
                 + jnp.dot(p_p.astype(BF16), v_ref[prows, :].astype(BF16), preferred_element_type=F32)) / l
            o_scr[g, rows, :] = o
            l_scr[g, rows, :] = jnp.broadcast_to(m + jnp.log(l), (QB, ATT_HEAD_DIM))
            return carry

        lax.fori_loop(0, dil * nblk, block, 0)

    l0, l1, l2 = l_scr[0], l_scr[1], l_scr[2]
    m = jnp.maximum(jnp.maximum(l0, l1), l2)
    e0, e1, e2 = jnp.exp(l0 - m), jnp.exp(l1 - m), jnp.exp(l2 - m)
    den = e0 + e1 + e2
    out_ref[...] = ((e0 * o_scr[0] + e1 * o_scr[1] + e2 * o_scr[2]) / den).astype(out_ref.dtype)


def _attn_prompt(z, slopes, *, nb, t):
    specs = []
    for g in range(len(ATT_GROUPS)):
        for c in range(3):
            cb = (g * 3 + c) * ATT_HEADS
            specs.append(pl.BlockSpec((t, ATT_HEAD_DIM), lambda b, h, sl, cb=cb: (b, cb + h)))
    grid_spec = pltpu.PrefetchScalarGridSpec(
        num_scalar_prefetch=1,
        grid=(nb, ATT_HEADS),
        in_specs=specs,
        out_specs=pl.BlockSpec((t, ATT_HEAD_DIM), lambda b, h, sl: (b, h)),
        scratch_shapes=[pltpu.VMEM((3, t, ATT_HEAD_DIM), F32), pltpu.VMEM((3, t, ATT_HEAD_DIM), F32)],
    )
    return pl.pallas_call(
        functools.partial(_attn_prompt_kernel, t=t),
        grid_spec=grid_spec,
        out_shape=jax.ShapeDtypeStruct((nb * t, ATT_HEADS * ATT_HEAD_DIM), BF16),
        compiler_params=_params(("parallel", "arbitrary"), 48),
        name="attn_prompt",
    )(slopes, *([z] * 9))


N_NEW = 8
N_BAND = 128


def _attn_sample_kernel(q0, q1, q2, n0, n1, n2, c0, c1, c2, sl_ref, out_ref, sc_scr, o_scr, l_scr):
    scale = ATT_HEAD_DIM ** -0.5
    qs, news, caches = (q0, q1, q2), (n0, n1, n2), (c0, c1, c2)
    tile = (ATT_HEADS, ATT_HEAD_DIM)
    for g, (win, dil) in enumerate(ATT_GROUPS):
        q_ref, n_ref, c_ref = qs[g], news[g], caches[g]
        slope = sl_ref[g]
        past = N_BAND * dil
        qv = [q_ref[s] * scale for s in range(N_NEW)]
        if dil == 1:
            combos = [(0, s) for s in range(N_NEW)]
        elif dil == 4:
            combos = [(r, s) for r in range(4) for s in (r, r + 4)]
        else:
            combos = [(r, r) for r in range(N_NEW)]

        def row_tiles(m, r):
            if dil == 1:
                return c_ref[m, 0], c_ref[m, 1]
            if dil == 4:
                a = m * 4 + r
                return c_ref[a, 0], c_ref[a, 1]
            return c_ref[m, r, 0], c_ref[m, r, 1]

        def score(m, r, s):
            a = m * dil + r
            k_t, _ = row_tiles(m, r)
            sc = jnp.sum(qv[s] * k_t, axis=-1, keepdims=True) - slope * (past + s - a).astype(F32)
            if dil == 1:
                sc = jnp.where(m >= s, sc, NEG_INF)
            elif dil == 4 and s >= 4:
                sc = jnp.where(m >= 1, sc, NEG_INF)
            return jnp.broadcast_to(sc, tile)

        new_sc = []
        for s in range(N_NEW):
            lst = []
            for j in range(s // dil + 1):
                s2 = s - j * dil
                sc = jnp.sum(qv[s] * n_ref[s2, 0], axis=-1, keepdims=True) - slope * float(j * dil)
                lst.append((s2, jnp.broadcast_to(sc, tile)))
            new_sc.append(lst)
        m0 = [functools.reduce(jnp.maximum, [x for _, x in new_sc[s]]) for s in range(N_NEW)]

        def pass1(m, mx):
            mx = list(mx)
            for ci, (r, s) in enumerate(combos):
                sc = score(m, r, s)
                sc_scr[m, ci] = sc
                mx[s] = jnp.maximum(mx[s], sc)
            return tuple(mx)

        mx = lax.fori_loop(0, N_BAND, pass1, tuple(m0), unroll=32)

        l_init, o_init = [], []
        for s in range(N_NEW):
            l_s = jnp.zeros(tile, F32)
            o_s = jnp.zeros(tile, F32)
            for s2, sc in new_sc[s]:
                p = jnp.exp(sc - mx[s])
                l_s = l_s + p
                o_s = o_s + p * n_ref[s2, 1]
            l_init.append(l_s)
            o_init.append(o_s)

        def pass2(m, carry):
            ls, os_ = list(carry[0]), list(carry[1])
            for ci, (r, s) in enumerate(combos):
                _, v_t = row_tiles(m, r)
                p = jnp.exp(sc_scr[m, ci] - mx[s])
                ls[s] = ls[s] + p
                os_[s] = os_[s] + p * v_t
            return tuple(ls), tuple(os_)

        ls, os_ = lax.fori_loop(0, N_BAND, pass2, (tuple(l_init), tuple(o_init)), unroll=8)
        for s in range(N_NEW):
            o_scr[g, s] = os_[s] / ls[s]
            l_scr[g, s] = mx[s] + jnp.log(ls[s])

    for s in range(N_NEW):
        l0, l1, l2 = l_scr[0, s], l_scr[1, s], l_scr[2, s]
        m = jnp.maximum(jnp.maximum(l0, l1), l2)
        e0, e1, e2 = jnp.exp(l0 - m), jnp.exp(l1 - m), jnp.exp(l2 - m)
        out_ref[s] = (e0 * o_scr[0, s] + e1 * o_scr[1, s] + e2 * o_scr[2, s]) / (e0 + e1 + e2)


def _attn_sample(qs, news, caches, slope_tiles):
    nb = qs[0].shape[0]
    he = (ATT_HEADS, ATT_HEAD_DIM)
    qspec = pl.BlockSpec((None, N_NEW) + he, lambda b: (b, 0, 0, 0))
    nspec = pl.BlockSpec((None, N_NEW, 2) + he, lambda b: (b, 0, 0, 0, 0))
    c2 = caches[2]
    dil2 = ATT_GROUPS[2][1]
    c2v = c2.reshape(nb, c2.shape[1] // dil2, dil2, 2, *he)
    cspecs = [
        pl.BlockSpec((None, caches[0].shape[1], 2) + he, lambda b: (b, 0, 0, 0, 0)),
        pl.BlockSpec((None, caches[1].shape[1], 2) + he, lambda b: (b, 0, 0, 0, 0)),
        pl.BlockSpec((None, N_BAND, N_NEW, 2) + he, lambda b: (b, 0, 0, 0, 0, 0)),
    ]
    return pl.pallas_call(
        _attn_sample_kernel,
        grid=(nb,),
        in_specs=[qspec] * 3 + [nspec] * 3 + cspecs + [pl.BlockSpec((3,) + he, lambda b: (0, 0, 0))],
        out_specs=pl.BlockSpec((None, N_NEW) + he, lambda b: (b, 0, 0, 0)),
        out_shape=jax.ShapeDtypeStruct((nb, N_NEW) + he, F32),
        scratch_shapes=[pltpu.VMEM((N_BAND, N_NEW) + he, F32), pltpu.VMEM((3, N_NEW) + he, F32),
                        pltpu.VMEM((3, N_NEW) + he, F32)],
        compiler_params=_params(("parallel",), 48),
        name="attn_sample",
    )(*qs, *news, caches[0], caches[1], c2v, slope_tiles)


def kernel(x_prompt, x_sample, cache_kv_0, cache_kv_1, cache_kv_2, state_wkv, state_shift, norm_mix, w_in, rw_mu,
           rw_w0, rw_w_up, rw_a0, rw_a_up, rw_g_up, rw_k_k, rw_k_a, rw_r_k, rw_lnx_w, rw_lnx_b, rw_out, att_out,
           w_o, norm_ffn, w_ffn_in, w_ffn_out, norm_final):
    depth = w_in.shape[0]
    assert depth == 1, "single-layer stack"
    bp, tp, d = x_prompt.shape
    bs, ts, _ = x_sample.shape
    mp, ms = bp * tp, bs * ts
    assert ts == N_NEW
    c = rw_w0.shape[1]
    r_dec, r_aaa, r_gate = rw_w_up.shape[1], rw_a_up.shape[1], rw_g_up.shape[1]
    rw_cols = 3 * c + r_dec + r_aaa + r_gate
    att_cols = len(ATT_GROUPS) * 3 * ATT_HEADS * ATT_HEAD_DIM
    att_w = ATT_HEADS * ATT_HEAD_DIM
    pad = (-(att_cols + rw_cols)) % LANES
    rwp = rw_cols + pad
    assert rwp - 3 * c == 512
    nz = att_cols + rwp

    tail = lambda w, off: jnp.zeros((512, c), F32).at[off:off + w.shape[0]].set(w).astype(BF16)
    row = lambda v, n: jnp.pad(v.reshape(1, -1), ((0, 0), (0, n - v.size)))
    seg = (jnp.arange(c // 2)[:, None] // RW_HEAD == jnp.arange(LANES)[None, :]).astype(BF16)
    prm = {
        "mu": row(rw_mu[0], rwp), "w0": row(rw_w0[0], c), "a0": row(rw_a0[0], c),
        "wup": tail(rw_w_up[0], 0), "aup": tail(rw_a_up[0], r_dec), "gup": tail(rw_g_up[0], r_dec + r_aaa),
        "k_k": row(rw_k_k[0], c), "k_a": row(rw_k_a[0], c), "r_k": row(rw_r_k[0].reshape(-1), c),
        "seg": seg, "seg_t": seg.T,
    }
    seg_full = (jnp.arange(c)[:, None] // RW_HEAD == jnp.arange(LANES)[None, :]).astype(BF16)

    x = jnp.concatenate([x_prompt.reshape(mp, d), x_sample.reshape(ms, d)], axis=0)
    m_all = mp + ms
    hmix = _rmsnorm(x, norm_mix[0], BF16)
    wi = w_in[0]
    z = _matmul(hmix, wi, tm=1024, tn=512, n_out=nz, vmem_mib=56, name="w_in")
    zg = _matmul(hmix, wi[:, att_cols + rw_cols:], tm=1024, tn=512, vmem_mib=56, name="w_gate")
    z_s = z[mp:].reshape(bs, ts, nz)

    new_kv_p, new_kv_s = [], []
    for g, (win, dil) in enumerate(ATT_GROUPS):
        lo = g * 3 * att_w + att_w
        keep = min(win, tp)
        kv = jnp.stack([z[(b + 1) * tp - keep:(b + 1) * tp, lo:lo + 2 * att_w] for b in range(bp)])
        new_kv_p.append(kv.reshape(bp, keep, 2, ATT_HEADS, ATT_HEAD_DIM)[None])
        new_kv_s.append(z_s[:, :, lo:lo + 2 * att_w].reshape(bs, ts, 2, ATT_HEADS, ATT_HEAD_DIM)[None])
    shift_p = jnp.stack([z[(b + 1) * tp - 1, att_cols:att_cols + rw_cols] for b in range(bp)])[None]
    shift_s = z_s[:, -1, att_cols:att_cols + rw_cols][None]

    tm_prep = 256
    prep_p = _rwkv_prep(z, 0, mp, att_cols, None, jnp.zeros((bp, 1, rwp), F32), tp, prm, tm=tm_prep)
    cols_s = z_s[:, :, att_cols:att_cols + rwp]
    shift_in = jnp.pad(state_shift[0], ((0, 0), (0, pad)))
    prev_s = jnp.concatenate([shift_in[:, None, :], cols_s[:, :-1]], axis=1).reshape(ms, rwp)
    prep_s = _rwkv_prep(z, mp, ms, att_cols, prev_s, None, ts, prm, tm=tm_prep)

    h_rw = c // RW_HEAD
    y_p, wkv_p = _wkv_scan(*prep_p[:6], jnp.zeros((bp, h_rw, RW_HEAD, RW_HEAD), F32), nb=bp, t=tp, tc=64)
    y_s, wkv_s = _wkv_scan(*prep_s[:6], state_wkv[0], nb=bs, t=ts, tc=ts)
    lw, lb = rw_lnx_w[0].reshape(1, c), rw_lnx_b[0].reshape(1, c)
    rw_p = _rw_post(y_p, prep_p[6], prep_p[7], lw, lb, seg_full, seg_full.T, tm=256)
    rw_s = _rw_post(y_s, prep_s[6], prep_s[7], lw, lb, seg_full, seg_full.T, tm=256)
    rw_act = jnp.concatenate([rw_p, rw_s], axis=0)

    n_heads_all = len(ATT_GROUPS) * ATT_HEADS
    slopes = 2.0 ** (-8.0 * jnp.arange(1, n_heads_all + 1, dtype=F32) / n_heads_all)
    att_p = _attn_prompt(z, slopes, nb=bp, t=tp)
    qs = [z_s[:, :, g * 3 * att_w:g * 3 * att_w + att_w].reshape(bs, ts, ATT_HEADS, ATT_HEAD_DIM)
          for g in range(len(ATT_GROUPS))]
    slope_tiles = jnp.broadcast_to(slopes.reshape(len(ATT_GROUPS), ATT_HEADS, 1),
                                   (len(ATT_GROUPS), ATT_HEADS, ATT_HEAD_DIM))
    att_s = _attn_sample(qs, [kv[0] for kv in new_kv_s], [cache_kv_0[0], cache_kv_1[0], cache_kv_2[0]], slope_tiles)
    att_act = jnp.concatenate([att_p, att_s.reshape(ms, att_w).astype(BF16)], axis=0)

    mixed = _mix(rw_act, att_act, rw_out[0].astype(BF16), att_out[0].astype(BF16), zg, 0, tm=1024, tn=512)
    x1 = _matmul(mixed, w_o[0].astype(BF16), tm=1024, tn=512, residual=x, name="w_o")
    hf = _rmsnorm(x1, norm_ffn[0], BF16)
    act = _swiglu(hf, w_ffn_in[0], tm=1024, tn=256)
    x2 = _matmul(act, w_ffn_out[0].astype(BF16), tm=512, tn=256, residual=x1, vmem_mib=56, name="w_ffn_out")
    y_p = _rmsnorm(x2, norm_final, F32, 0, mp)
    y_s = _rmsnorm(x2, norm_final, F32, mp, ms)

    return (y_p.reshape(bp, tp, d), y_s.reshape(bs, ts, d),
            new_kv_p[0], new_kv_p[1], new_kv_p[2], wkv_p[None], shift_p,
            new_kv_s[0], new_kv_s[1], new_kv_s[2], wkv_s[None], shift_s)
```

```python
import functools

import jax
import jax.numpy as jnp
from jax import lax
from jax.experimental import pallas as pl
from jax.experimental.pallas import tpu as pltpu

F32 = jnp.float32
BF16 = jnp.bfloat16

RMS_EPS = 1e-6
GN_EPS = 64e-5
RW_HEAD = 64
ATT_GROUPS = ((128, 1), (512, 4), (2048, 16))
ATT_HEAD_DIM = 128
ATT_HEADS = 8
LANES = 128
SUBLANES = 8
NEG_INF = float("-inf")


def _params(sem, vmem_mib):
    return pltpu.CompilerParams(dimension_semantics=sem, vmem_limit_bytes=vmem_mib << 20)


def _rmsnorm_kernel(x_ref, g_ref, o_ref):
    x = x_ref[...]
    ms = jnp.mean(x * x, axis=-1, keepdims=True)
    o_ref[...] = (x * lax.rsqrt(ms + RMS_EPS) * g_ref[...]).astype(o_ref.dtype)


def _rmsnorm(x, g, out_dtype, row0=0, rows=None, tm=256):
    d = x.shape[1]
    rows = x.shape[0] if rows is None else rows
    rb0 = row0 // tm
    return pl.pallas_call(
        _rmsnorm_kernel,
        grid=(rows // tm,),
        in_specs=[pl.BlockSpec((tm, d), lambda i: (rb0 + i, 0)), pl.BlockSpec((1, d), lambda i: (0, 0))],
        out_specs=pl.BlockSpec((tm, d), lambda i: (i, 0)),
        out_shape=jax.ShapeDtypeStruct((rows, d), out_dtype),
        compiler_params=_params(("parallel",), 32),
        name="rmsnorm",
    )(x, g.reshape(1, d))


def _mm_kernel(a_ref, w_ref, o_ref):
    o_ref[...] = jnp.dot(a_ref[...], w_ref[...].astype(BF16), preferred_element_type=F32).astype(o_ref.dtype)


def _mm_res_kernel(a_ref, w_ref, r_ref, o_ref):
    o_ref[...] = r_ref[...] + jnp.dot(a_ref[...], w_ref[...].astype(BF16), preferred_element_type=F32)


def _matmul(a, w, *, tm, tn, n_out=None, residual=None, vmem_mib=48, name="matmul"):
    m, k = a.shape
    n = w.shape[-1] if n_out is None else n_out
    if w.ndim == 3:
        w_spec = pl.BlockSpec((None, k, tn), lambda i, j: (0, 0, j))
    else:
        w_spec = pl.BlockSpec((k, tn), lambda i, j: (0, j))
    in_specs = [pl.BlockSpec((tm, k), lambda i, j: (i, 0)), w_spec]
    args = [a, w]
    kern = _mm_kernel
    if residual is not None:
        in_specs.append(pl.BlockSpec((tm, tn), lambda i, j: (i, j)))
        args.append(residual)
        kern = _mm_res_kernel
    return pl.pallas_call(
        kern,
        grid=(m // tm, n // tn),
        in_specs=in_specs,
        out_specs=pl.BlockSpec((tm, tn), lambda i, j: (i, j)),
        out_shape=jax.ShapeDtypeStruct((m, n), F32),
        compiler_params=_params(("parallel", "arbitrary"), vmem_mib),
        name=name,
    )(*args)


def _swiglu_kernel(a_ref, wg_ref, wu_ref, o_ref):
    a = a_ref[...]
    g = jnp.dot(a, wg_ref[...].astype(BF16), preferred_element_type=F32)
    u = jnp.dot(a, wu_ref[...].astype(BF16), preferred_element_type=F32)
    o_ref[...] = (g * jax.nn.sigmoid(g) * u).astype(o_ref.dtype)


def _swiglu(a, w, *, tm, tn):
    m, k = a.shape
    f = w.shape[1] // 2
    nb = f // tn
    return pl.pallas_call(
        _swiglu_kernel,
        grid=(m // tm, nb),
        in_specs=[
            pl.BlockSpec((tm, k), lambda i, j: (i, 0)),
            pl.BlockSpec((k, tn), lambda i, j: (0, j)),
            pl.BlockSpec((k, tn), lambda i, j: (0, j + nb)),
        ],
        out_specs=pl.BlockSpec((tm, tn), lambda i, j: (i, j)),
        out_shape=jax.ShapeDtypeStruct((m, f), BF16),
        compiler_params=_params(("parallel", "arbitrary"), 56),
        name="swiglu",
    )(a, w, w)


def _mix_kernel(rw_ref, at_ref, wr_ref, wa_ref, ga_ref, gb_ref, o_ref):
    rw = jnp.dot(rw_ref[...], wr_ref[...], preferred_element_type=F32)
    at = jnp.dot(at_ref[...], wa_ref[...], preferred_element_type=F32)
    mixed = jax.nn.sigmoid(ga_ref[...]) * rw + jax.nn.sigmoid(gb_ref[...]) * at
    o_ref[...] = mixed.astype(o_ref.dtype)


def _mix(rw, at, w_rw, w_at, z, gate_col0, *, tm, tn):
    m, kr = rw.shape
    ka = at.shape[1]
    d = w_rw.shape[1]
    nb = d // tn
    g0 = gate_col0 // tn
    return pl.pallas_call(
        _mix_kernel,
        grid=(m // tm, nb),
        in_specs=[
            pl.BlockSpec((tm, kr), lambda i, j: (i, 0)),
            pl.BlockSpec((tm, ka), lambda i, j: (i, 0)),
            pl.BlockSpec((kr, tn), lambda i, j: (0, j)),
            pl.BlockSpec((ka, tn), lambda i, j: (0, j)),
            pl.BlockSpec((tm, tn), lambda i, j: (i, g0 + j)),
            pl.BlockSpec((tm, tn), lambda i, j: (i, g0 + nb + j)),
        ],
        out_specs=pl.BlockSpec((tm, tn), lambda i, j: (i, j)),
        out_shape=jax.ShapeDtypeStruct((m, d), BF16),
        compiler_params=_params(("parallel", "arbitrary"), 48),
        name="mix",
    )(rw, at, w_rw, w_at, z, z)


def _split_bf16(x):
    hi = x.astype(BF16)
    lo = (x - hi.astype(F32)).astype(BF16)
    return hi, lo


def _seg_sum_bcast(x, seg, seg_t):
    hi, lo = _split_bf16(x)
    s = jnp.dot(hi, seg, preferred_element_type=F32) + jnp.dot(lo, seg, preferred_element_type=F32)
    shi, slo = _split_bf16(s)
    return jnp.dot(shi, seg_t, preferred_element_type=F32) + jnp.dot(slo, seg_t, preferred_element_type=F32)


def _prep_body(cur, prev, mu, w0, wup, a0, aup, gup, k_k, k_a, r_k, seg, seg_t, outs):
    r_ref, w_ref, k_ref, v_ref, kkn_ref, b_ref, g_ref, bonus_ref = outs
    xs = {n: cur[n] + (prev[n] - cur[n]) * mu[n] for n in ("r", "k", "v", "t")}
    r, k, v, t = xs["r"], xs["k"], xs["v"], xs["t"]
    wl = w0 + jnp.dot(jnp.tanh(t).astype(BF16), wup, preferred_element_type=F32)
    w_log = -(jnp.maximum(-wl, 0.0) + jnp.log(1.0 + jnp.exp(-jnp.abs(wl)))) - 0.5
    decay = jnp.exp(-jnp.exp(w_log))
    a = jax.nn.sigmoid(a0 + jnp.dot(t.astype(BF16), aup, preferred_element_type=F32))
    g = jnp.dot(jax.nn.sigmoid(t).astype(BF16), gup, preferred_element_type=F32)
    kk = k * k_k
    kk = kk * lax.rsqrt(jnp.maximum(_seg_sum_bcast(kk * kk, seg, seg_t), 1e-24))
    k_mod = k * (1.0 + (a - 1.0) * k_a)
    bonus = _seg_sum_bcast(r * k_mod * r_k, seg, seg_t) * v
    r_ref[...] = r
    w_ref[...] = decay
    k_ref[...] = k_mod
    v_ref[...] = v
    kkn_ref[...] = -kk
    b_ref[...] = kk * a
    g_ref[...] = g
    bonus_ref[...] = bonus


def _prep_roll_kernel(zr, zk, zv, zt, br, bk, bv, bt, sr, sk, sv, st, mr, mk, mv, mt, w0, wup, a0, aup, gup, k_k,
                      k_a, r_k, seg, seg_t, *outs, tiles_per_seq):
    cur = {"r": zr[...], "k": zk[...], "v": zv[...], "t": zt[...]}
    seq_start = pl.program_id(0) % tiles_per_seq == 0
    prev = {}
    for n, bref, sref in (("r", br, sr), ("k", bk, sk), ("v", bv, sv), ("t", bt, st)):
        x = cur[n]
        rolled = pltpu.roll(x, 1, 0)
        row0 = lax.broadcasted_iota(jnp.int32, x.shape, 0) == 0
        first = jnp.where(seq_start, sref[...], bref[SUBLANES - 1:, :])
        prev[n] = jnp.where(row0, first, rolled)
    mu = {"r": mr[...], "k": mk[...], "v": mv[...], "t": mt[...]}
    _prep_body(cur, prev, mu, w0[...], wup[...], a0[...], aup[...], gup[...], k_k[...], k_a[...], r_k[...],
               seg[...], seg_t[...], outs)


def _prep_given_kernel(zr, zk, zv, zt, pr, pk, pv, pt, mr, mk, mv, mt, w0, wup, a0, aup, gup, k_k, k_a, r_k, seg,
                       seg_t, *outs):
    cur = {"r": zr[...], "k": zk[...], "v": zv[...], "t": zt[...]}
    prev = {"r": pr[...], "k": pk[...], "v": pv[...], "t": pt[...]}
    mu = {"r": mr[...], "k": mk[...], "v": mv[...], "t": mt[...]}
    _prep_body(cur, prev, mu, w0[...], wup[...], a0[...], aup[...], gup[...], k_k[...], k_a[...], r_k[...],
               seg[...], seg_t[...], outs)


def _rwkv_prep(z, row0, rows, rw_col0, prev_arr, shift_arr, seq_len, p, *, tm):
    c = p["w0"].shape[1]
    hc = c // 2
    nrb = rows // tm
    rb0 = row0 // tm
    cb = rw_col0 // hc
    tb = (rw_col0 + 3 * c) // 512
    zspec = lambda off: pl.BlockSpec((tm, hc), lambda i, j: (rb0 + i, cb + off + j))
    in_specs = [zspec(0), zspec(2), zspec(4), pl.BlockSpec((tm, 512), lambda i, j: (rb0 + i, tb))]
    args = [z, z, z, z]
    if prev_arr is not None:
        kern = _prep_given_kernel
        pspec = lambda off: pl.BlockSpec((tm, hc), lambda i, j: (i, off + j))
        in_specs += [pspec(0), pspec(2), pspec(4), pl.BlockSpec((tm, 512), lambda i, j: (i, 3 * c // 512))]
        args += [prev_arr] * 4
    else:
        tiles_per_seq = seq_len // tm
        kern = functools.partial(_prep_roll_kernel, tiles_per_seq=tiles_per_seq)
        before = lambda i: jnp.maximum((rb0 + i) * (tm // SUBLANES) - 1, 0)
        bspec = lambda off: pl.BlockSpec((SUBLANES, hc), lambda i, j: (before(i), cb + off + j))
        in_specs += [bspec(0), bspec(2), bspec(4), pl.BlockSpec((SUBLANES, 512), lambda i, j: (before(i), tb))]
        args += [z] * 4
        sspec = lambda off: pl.BlockSpec((None, 1, hc), lambda i, j: (i // tiles_per_seq, 0, off + j))
        in_specs += [sspec(0), sspec(2), sspec(4),
                     pl.BlockSpec((None, 1, 512), lambda i, j: (i // tiles_per_seq, 0, 3 * c // 512))]
        args += [shift_arr] * 4
    mspec = lambda off: pl.BlockSpec((1, hc), lambda i, j: (0, off + j))
    in_specs += [mspec(0), mspec(2), mspec(4), pl.BlockSpec((1, 512), lambda i, j: (0, 3 * c // 512))]
    args += [p["mu"]] * 4
    vec = pl.BlockSpec((1, hc), lambda i, j: (0, j))
    lowrank = pl.BlockSpec((512, hc), lambda i, j: (0, j))
    in_specs += [vec, lowrank, vec, lowrank, lowrank, vec, vec, vec,
                 pl.BlockSpec((hc, LANES), lambda i, j: (0, 0)), pl.BlockSpec((LANES, hc), lambda i, j: (0, 0))]
    args += [p["w0"], p["wup"], p["a0"], p["aup"], p["gup"], p["k_k"], p["k_a"], p["r_k"], p["seg"], p["seg_t"]]
    out_spec = pl.BlockSpec((tm, hc), lambda i, j: (i, j))
    return pl.pallas_call(
        kern,
        grid=(nrb, 2),
        in_specs=in_specs,
        out_specs=[out_spec] * 8,
        out_shape=[jax.ShapeDtypeStruct((rows, c), F32)] * 8,
        compiler_params=_params(("parallel", "arbitrary"), 48),
        name="rwkv_prep",
    )(*args)


N_PAIRS = 16
N_V_MXU = 0


def _trunc_bf16(x):
    return lax.bitcast_convert_type(lax.bitcast_convert_type(x, jnp.int32) & jnp.int32(-65536), F32)


def _wkv_kernel(r_ref, w_ref, k_ref, v_ref, kkn_ref, b_ref, s0_ref, y_ref, st_ref, s_scr, y_scr, *, tc):
    ci = pl.program_id(1)

    @pl.when(ci == 0)
    def _():
        for p in range(N_PAIRS):
            s_scr[p] = jnp.concatenate([s0_ref[2 * p], s0_ref[2 * p + 1]], axis=1)

    y_scr[...] = jnp.zeros(y_scr.shape, F32)
    lane = lax.broadcasted_iota(jnp.int32, (RW_HEAD, LANES), 1)
    sub = lax.broadcasted_iota(jnp.int32, (RW_HEAD, LANES), 0)
    head_a = lane < RW_HEAD
    diag_a = lane == sub
    diag_b = lane == sub + RW_HEAD
    diag = (lane & (RW_HEAD - 1)) == sub
    n_v = N_V_MXU * RW_HEAD
    wr = lax.broadcasted_iota(jnp.int32, (2 * LANES, 2 * LANES), 0)
    wc = lax.broadcasted_iota(jnp.int32, (2 * LANES, 2 * LANES), 1)
    ones_bd = ((wr // RW_HEAD) == (wc // RW_HEAD)).astype(F32)
    n_sa = N_PAIRS * RW_HEAD

    def readout(res, ymask):
        for p in range(N_PAIRS):
            blk = n_sa + RW_HEAD * (p // 2)
            yb = res[blk:blk + RW_HEAD, LANES * (p % 2):LANES * (p % 2 + 1)]
            y_scr[p] = jnp.where(ymask, yb, y_scr[p])

    def step_group(gi, carry):
        t0 = pl.multiple_of(gi * SUBLANES, SUBLANES)
        tp0 = pl.multiple_of(jnp.maximum(gi - 1, 0) * SUBLANES, SUBLANES)
        rows8 = lambda ref, p: ref[pl.ds(t0, SUBLANES), LANES * p:LANES * (p + 1)]
        kkn8 = [rows8(kkn_ref, p) for p in range(N_PAIRS)]
        r8 = [rows8(r_ref, p) for p in range(N_PAIRS)]
        v8 = [rows8(v_ref, p) for p in range(N_PAIRS)]
        w8 = [rows8(w_ref, p) for p in range(N_PAIRS)]
        b8 = [rows8(b_ref, p) for p in range(N_PAIRS)]
        k8 = [rows8(k_ref, p) for p in range(N_PAIRS)]
        r_last = [r_ref[pl.ds(tp0, SUBLANES), LANES * p:LANES * (p + 1)][SUBLANES - 1:] for p in range(N_PAIRS)]
        for u in range(SUBLANES):
            ymask = (lane & (RW_HEAD - 1)) == (t0 + (u - 1))
            lhs_sa, lhs_y, lhs_v, qs = [], [], [], []
            for p in range(N_PAIRS):
                s_p = s_scr[p]
                pr = s_p * kkn8[p][u:u + 1]
                pr_hi = _trunc_bf16(pr)
                lhs_sa.append(jnp.concatenate([pr_hi, pr - pr_hi], axis=1))
                qs.append(s_p * (r8[p][u - 1:u] if u else r_last[p]))
                if p % 2:
                    lhs_y.append(jnp.concatenate(qs[-2:], axis=1))
                if p < N_V_MXU:
                    vrow = v8[p][u:u + 1]
                    v_hi = _trunc_bf16(vrow)
                    lhs_v.append(jnp.concatenate(
                        [jnp.where(diag, v_hi, 0.0), jnp.where(diag, vrow - v_hi, 0.0)], axis=1))
            res = jnp.dot(jnp.concatenate(lhs_v + lhs_sa + lhs_y, axis=0), ones_bd, preferred_element_type=F32)
            res_v, res = res[:n_v], res[n_v:]
            for p in range(N_PAIRS):
                sa2 = res[RW_HEAD * p:RW_HEAD * (p + 1)]
                sa = sa2[:, :LANES] + sa2[:, LANES:]
                if p < N_V_MXU:
                    v2 = res_v[RW_HEAD * p:RW_HEAD * (p + 1)]
                    vcol = v2[:, :LANES] + v2[:, LANES:]
                else:
                    vrow = v8[p][u:u + 1]
                    va = jnp.sum(jnp.where(diag_a, vrow, 0.0), axis=1, keepdims=True)
                    vb = jnp.sum(jnp.where(diag_b, vrow, 0.0), axis=1, keepdims=True)
                    vcol = jnp.where(head_a, va, vb)
                s_scr[p] = s_scr[p] * w8[p][u:u + 1] + sa * b8[p][u:u + 1] + vcol * k8[p][u:u + 1]
            readout(res, ymask)
        return carry

    lax.fori_loop(0, tc // SUBLANES, step_group, 0)

    qs = [s_scr[p] * r_ref[pl.ds(tc - 1, 1), slice(LANES * p, LANES * (p + 1))] for p in range(N_PAIRS)]
    lhs_y = [jnp.concatenate(qs[p:p + 2], axis=1) for p in range(0, N_PAIRS, 2)]
    res = jnp.dot(jnp.concatenate(lhs_y, axis=0), ones_bd, preferred_element_type=F32)
    readout(jnp.concatenate([jnp.zeros((n_sa, 2 * LANES), F32), res], axis=0), (lane & (RW_HEAD - 1)) == (tc - 1))
    for p in range(N_PAIRS):
        yt = y_scr[p].T
        rows = jnp.concatenate([yt[:RW_HEAD], yt[RW_HEAD:]], axis=1)
        y_ref[:, LANES * p:LANES * (p + 1)] = rows[:tc]

    @pl.when(ci == pl.num_programs(1) - 1)
    def _():
        for p in range(N_PAIRS):
            s_p = s_scr[p]
            st_ref[2 * p] = s_p[:, :RW_HEAD]
            st_ref[2 * p + 1] = s_p[:, RW_HEAD:]


def _wkv_scan(r, w, k, v, kkn, b, s0, *, nb, t, tc):
    c = r.shape[1]
    h = c // RW_HEAD
    nc = t // tc
    row = pl.BlockSpec((tc, c), lambda bi, ci: (bi * nc + ci, 0))
    st = pl.BlockSpec((None, None, h, RW_HEAD, RW_HEAD), lambda bi, ci: (0, bi, 0, 0, 0))
    return pl.pallas_call(
        functools.partial(_wkv_kernel, tc=tc),
        grid=(nb, nc),
        in_specs=[row] * 6 + [st],
        out_specs=[row, st],
        out_shape=[jax.ShapeDtypeStruct((nb * t, c), F32), jax.ShapeDtypeStruct((1, nb, h, RW_HEAD, RW_HEAD), F32)],
        scratch_shapes=[pltpu.VMEM((N_PAIRS, RW_HEAD, LANES), F32), pltpu.VMEM((N_PAIRS, RW_HEAD, LANES), F32)],
        compiler_params=_params(("parallel", "arbitrary"), 48),
        name="wkv_scan",
    )(r, w, k, v, kkn, b, s0)


def _rw_post_kernel(y_ref, g_ref, bonus_ref, lw_ref, lb_ref, seg_ref, segt_ref, o_ref):
    y = y_ref[...]
    seg = seg_ref[...]
    seg_t = segt_ref[...]
    inv = 1.0 / RW_HEAD
    mu = _seg_sum_bcast(y, seg, seg_t) * inv
    d = y - mu
    var = _seg_sum_bcast(d * d, seg, seg_t) * inv
    yn = d * lax.rsqrt(var + GN_EPS)
    o_ref[...] = ((yn * lw_ref[...] + lb_ref[...] + bonus_ref[...]) * g_ref[...]).astype(o_ref.dtype)


def _rw_post(y, g, bonus, lnx_w, lnx_b, seg, seg_t, *, tm):
    m, c = y.shape
    blk = pl.BlockSpec((tm, c), lambda i: (i, 0))
    vec = pl.BlockSpec((1, c), lambda i: (0, 0))
    return pl.pallas_call(
        _rw_post_kernel,
        grid=(m // tm,),
        in_specs=[blk, blk, blk, vec, vec, pl.BlockSpec((c, LANES), lambda i: (0, 0)),
                  pl.BlockSpec((LANES, c), lambda i: (0, 0))],
        out_specs=blk,
        out_shape=jax.ShapeDtypeStruct((m, c), BF16),
        compiler_params=_params(("parallel",), 48),
        name="rw_post",
    )(y, g, bonus, lnx_w, lnx_b, seg, seg_t)


QB = 128


def _attn_prompt_kernel(sl_ref, q0, k0, v0, q1, k1, v1, q2, k2, v2, out_ref, o_scr, l_scr, *, t):
    h = pl.program_id(1)
    scale = ATT_HEAD_DIM ** -0.5
    iq = lax.broadcasted_iota(jnp.int32, (QB, QB), 0)
    ik = lax.broadcasted_iota(jnp.int32, (QB, QB), 1)
    d_cur = (iq - ik).astype(F32)
    refs = ((q0, k0, v0), (q1, k1, v1), (q2, k2, v2))
    for g, (win, dil) in enumerate(ATT_GROUPS):
        q_ref, k_ref, v_ref = refs[g]
        slope = sl_ref[g * ATT_HEADS + h]
        nblk = t // dil // QB
        bias_cur = jnp.where(ik <= iq, -slope * dil * d_cur, NEG_INF)
        bias_prev = jnp.where(ik >= iq, -slope * dil * (d_cur + QB), NEG_INF)

        def block(idx, carry, q_ref=q_ref, k_ref=k_ref, v_ref=v_ref, dil=dil, nblk=nblk, g=g,
                  bias_cur=bias_cur, bias_prev=bias_prev):
            res = idx // nblk
            qb = idx % nblk
            base = res + qb * (QB * dil)
            pbase = res + jnp.maximum(qb - 1, 0) * (QB * dil)
            rows = pl.ds(base, QB, stride=dil) if dil > 1 else pl.ds(base, QB)
            prows = pl.ds(pbase, QB, stride=dil) if dil > 1 else pl.ds(pbase, QB)
            q = q_ref[rows, :].astype(BF16)
            kc = k_ref[rows, :].astype(BF16)
            dims = (((1,), (1,)), ((), ()))
            s_c = lax.dot_general(q, kc, dims, preferred_element_type=F32) * scale + bias_cur
            m = jnp.max(s_c, axis=-1, keepdims=True)
            if nblk > 1:
                kp = k_ref[prows, :].astype(BF16)
                s_p = lax.dot_general(q, kp, dims, preferred_element_type=F32) * scale + jnp.where(
                    qb > 0, bias_prev, NEG_INF)
                m = jnp.maximum(m, jnp.max(s_p, axis=-1, keepdims=True))
            p_c = jnp.exp(s_c - m)
            l = jnp.sum(p_c, axis=-1, keepdims=True)
            o = jnp.dot(p_c.astype(BF16), v_ref[rows, :].astype(BF16), preferred_element_type=F32)
            if nblk > 1:
                p_p = jnp.exp(s_p - m)
                l = l + jnp.sum(p_p, axis=-1, keepdims=True)
                o = o + jnp.dot(p_p.astype(BF16), v_ref[prows, :].astype(BF16), preferred_element_type=F32)
            o_scr[g, rows, :] = o / l
            l_scr[g, rows, :] = jnp.broadcast_to(m + jnp.log(l), (QB, ATT_HEAD_DIM))
            return carry

        lax.fori_loop(0, dil * nblk, block, 0, unroll=4)

    l0, l1, l2 = l_scr[0], l_scr[1], l_scr[2]
    m = jnp.maximum(jnp.maximum(l0, l1), l2)
    e0, e1, e2 = jnp.exp(l0 - m), jnp.exp(l1 - m), jnp.exp(l2 - m)
    den = e0 + e1 + e2
    out_ref[...] = ((e0 * o_scr[0] + e1 * o_scr[1] + e2 * o_scr[2]) / den).astype(out_ref.dtype)


def _attn_prompt(z, slopes, *, nb, t):
    specs = []
    for g in range(len(ATT_GROUPS)):
        for c in range(3):
            cb = (g * 3 + c) * ATT_HEADS
            specs.append(pl.BlockSpec((t, ATT_HEAD_DIM), lambda b, h, sl, cb=cb: (b, cb + h)))
    grid_spec = pltpu.PrefetchScalarGridSpec(
        num_scalar_prefetch=1,
        grid=(nb, ATT_HEADS),
        in_specs=specs,
        out_specs=pl.BlockSpec((t, ATT_HEAD_DIM), lambda b, h, sl: (b, h)),
        scratch_shapes=[pltpu.VMEM((3, t, ATT_HEAD_DIM), F32), pltpu.VMEM((3, t, ATT_HEAD_DIM), F32)],
    )
    return pl.pallas_call(
        functools.partial(_attn_prompt_kernel, t=t),
        grid_spec=grid_spec,
        out_shape=jax.ShapeDtypeStruct((nb * t, ATT_HEADS * ATT_HEAD_DIM), BF16),
        compiler_params=_params(("parallel", "arbitrary"), 48),
        name="attn_prompt",
    )(slopes, *([z] * 9))


N_NEW = 8
N_BAND = 128


def _attn_sample_kernel(q0, q1, q2, n0, n1, n2, c0, c1, c2, sl_ref, out_ref, sc_scr, o_scr, l_scr):
    scale = ATT_HEAD_DIM ** -0.5
    qs, news, caches = (q0, q1, q2), (n0, n1, n2), (c0, c1, c2)
    tile = (ATT_HEADS, ATT_HEAD_DIM)
    for g, (win, dil) in enumerate(ATT_GROUPS):
        q_ref, n_ref, c_ref = qs[g], news[g], caches[g]
        slope = sl_ref[g]
        past = N_BAND * dil
        qv = [q_ref[s] * scale for s in range(N_NEW)]
        if dil == 1:
            combos = [(0, s) for s in range(N_NEW)]
        elif dil == 4:
            combos = [(r, s) for r in range(4) for s in (r, r + 4)]
        else:
            combos = [(r, r) for r in range(N_NEW)]

        def row_tiles(m, r):
            if dil == 1:
                return c_ref[m, 0], c_ref[m, 1]
            if dil == 4:
                a = m * 4 + r
                return c_ref[a, 0], c_ref[a, 1]
            return c_ref[m, r, 0], c_ref[m, r, 1]

        def score(m, r, s):
            a = m * dil + r
            k_t, _ = row_tiles(m, r)
            sc = jnp.sum(qv[s] * k_t, axis=-1, keepdims=True) - slope * jnp.asarray(past + s - a, F32)
            if dil == 1:
                sc = jnp.where(m >= s, sc, NEG_INF)
            elif dil == 4 and s >= 4:
                sc = jnp.where(m >= 1, sc, NEG_INF)
            return jnp.broadcast_to(sc, tile)

        new_sc = []
        for s in range(N_NEW):
            lst = []
            for j in range(s // dil + 1):
                s2 = s - j * dil
                sc = jnp.sum(qv[s] * n_ref[s2, 0], axis=-1, keepdims=True) - slope * float(j * dil)
                lst.append((s2, jnp.broadcast_to(sc, tile)))
            new_sc.append(lst)
        m0 = [functools.reduce(jnp.maximum, [x for _, x in new_sc[s]]) for s in range(N_NEW)]

        def pass1(m, mx):
            mx = list(mx)
            for ci, (r, s) in enumerate(combos):
                sc = score(m, r, s)
                sc_scr[m, ci] = sc
                mx[s] = jnp.maximum(mx[s], sc)
            return tuple(mx)

        mx = lax.fori_loop(0, N_BAND, pass1, tuple(m0), unroll=32)

        l_init, o_init = [], []
        for s in range(N_NEW):
            l_s = jnp.zeros(tile, F32)
            o_s = jnp.zeros(tile, F32)
            for s2, sc in new_sc[s]:
                p = jnp.exp(sc - mx[s])
                l_s = l_s + p
                o_s = o_s + p * n_ref[s2, 1]
            l_init.append(l_s)
            o_init.append(o_s)

        def pass2(m, carry):
            ls, os_ = list(carry[0]), list(carry[1])
            for ci, (r, s) in enumerate(combos):
                _, v_t = row_tiles(m, r)
                p = jnp.exp(sc_scr[m, ci] - mx[s])
                ls[s] = ls[s] + p
                os_[s] = os_[s] + p * v_t
            return tuple(ls), tuple(os_)

        ls, os_ = lax.fori_loop(0, N_BAND, pass2, (tuple(l_init), tuple(o_init)), unroll=8)
        for s in range(N_NEW):
            o_scr[g, s] = os_[s] / ls[s]
            l_scr[g, s] = mx[s] + jnp.log(ls[s])

    for s in range(N_NEW):
        l0, l1, l2 = l_scr[0, s], l_scr[1, s], l_scr[2, s]
        m = jnp.maximum(jnp.maximum(l0, l1), l2)
        e0, e1, e2 = jnp.exp(l0 - m), jnp.exp(l1 - m), jnp.exp(l2 - m)
        out_ref[s] = (e0 * o_scr[0, s] + e1 * o_scr[1, s] + e2 * o_scr[2, s]) / (e0 + e1 + e2)


def _attn_sample(qs, news, caches, slope_tiles):
    nb = qs[0].shape[0]
    he = (ATT_HEADS, ATT_HEAD_DIM)
    qspec = pl.BlockSpec((None, N_NEW) + he, lambda b: (b, 0, 0, 0))
    nspec = pl.BlockSpec((None, N_NEW, 2) + he, lambda b: (b, 0, 0, 0, 0))
    c2 = caches[2]
    dil2 = ATT_GROUPS[2][1]
    c2v = c2.reshape(nb, c2.shape[1] // dil2, dil2, 2, *he)
    cspecs = [
        pl.BlockSpec((None, caches[0].shape[1], 2) + he, lambda b: (b, 0, 0, 0, 0)),
        pl.BlockSpec((None, caches[1].shape[1], 2) + he, lambda b: (b, 0, 0, 0, 0)),
        pl.BlockSpec((None, N_BAND, N_NEW, 2) + he, lambda b: (b, 0, 0, 0, 0, 0)),
    ]
    return pl.pallas_call(
        _attn_sample_kernel,
        grid=(nb,),
        in_specs=[qspec] * 3 + [nspec] * 3 + cspecs + [pl.BlockSpec((3,) + he, lambda b: (0, 0, 0))],
        out_specs=pl.BlockSpec((None, N_NEW) + he, lambda b: (b, 0, 0, 0)),
        out_shape=jax.ShapeDtypeStruct((nb, N_NEW) + he, F32),
        scratch_shapes=[pltpu.VMEM((N_BAND, N_NEW) + he, F32), pltpu.VMEM((3, N_NEW) + he, F32),
                        pltpu.VMEM((3, N_NEW) + he, F32)],
        compiler_params=_params(("parallel",), 48),
        name="attn_sample",
    )(*qs, *news, caches[0], caches[1], c2v, slope_tiles)


def kernel(x_prompt, x_sample, cache_kv_0, cache_kv_1, cache_kv_2, state_wkv, state_shift, norm_mix, w_in, rw_mu,
           rw_w0, rw_w_up, rw_a0, rw_a_up, rw_g_up, rw_k_k, rw_k_a, rw_r_k, rw_lnx_w, rw_lnx_b, rw_out, att_out,
           w_o, norm_ffn, w_ffn_in, w_ffn_out, norm_final):
    depth = w_in.shape[0]
    assert depth == 1, "single-layer stack"
    bp, tp, d = x_prompt.shape
    bs, ts, _ = x_sample.shape
    mp, ms = bp * tp, bs * ts
    assert ts == N_NEW
    c = rw_w0.shape[1]
    r_dec, r_aaa, r_gate = rw_w_up.shape[1], rw_a_up.shape[1], rw_g_up.shape[1]
    rw_cols = 3 * c + r_dec + r_aaa + r_gate
    att_cols = len(ATT_GROUPS) * 3 * ATT_HEADS * ATT_HEAD_DIM
    att_w = ATT_HEADS * ATT_HEAD_DIM
    pad = (-(att_cols + rw_cols)) % LANES
    rwp = rw_cols + pad
    assert rwp - 3 * c == 512
    nz = att_cols + rwp

    tail = lambda w, off: jnp.zeros((512, c), F32).at[off:off + w.shape[0]].set(w).astype(BF16)
    row = lambda v, n: jnp.pad(v.reshape(1, -1), ((0, 0), (0, n - v.size)))
    seg = (jnp.arange(c // 2)[:, None] // RW_HEAD == jnp.arange(LANES)[None, :]).astype(BF16)
    prm = {
        "mu": row(rw_mu[0], rwp), "w0": row(rw_w0[0], c), "a0": row(rw_a0[0], c),
        "wup": tail(rw_w_up[0], 0), "aup": tail(rw_a_up[0], r_dec), "gup": tail(rw_g_up[0], r_dec + r_aaa),
        "k_k": row(rw_k_k[0], c), "k_a": row(rw_k_a[0], c), "r_k": row(rw_r_k[0].reshape(-1), c),
        "seg": seg, "seg_t": seg.T,
    }
    seg_full = (jnp.arange(c)[:, None] // RW_HEAD == jnp.arange(LANES)[None, :]).astype(BF16)

    x = jnp.concatenate([x_prompt.reshape(mp, d), x_sample.reshape(ms, d)], axis=0)
    m_all = mp + ms
    hmix = _rmsnorm(x, norm_mix[0], BF16)
    wi = w_in[0]
    z = _matmul(hmix, w_in, tm=1024, tn=512, n_out=nz, vmem_mib=56, name="w_in")
    zg = _matmul(hmix, wi[:, att_cols + rw_cols:], tm=1024, tn=512, vmem_mib=56, name="w_gate")
    z_s = z[mp:].reshape(bs, ts, nz)

    new_kv_p, new_kv_s = [], []
    for g, (win, dil) in enumerate(ATT_GROUPS):
        lo = g * 3 * att_w + att_w
        keep = min(win, tp)
        kv = jnp.stack([z[(b + 1) * tp - keep:(b + 1) * tp, lo:lo + 2 * att_w] for b in range(bp)])
        new_kv_p.append(kv.reshape(bp, keep, 2, ATT_HEADS, ATT_HEAD_DIM)[None])
        new_kv_s.append(z_s[:, :, lo:lo + 2 * att_w].reshape(bs, ts, 2, ATT_HEADS, ATT_HEAD_DIM)[None])
    shift_p = jnp.stack([z[(b + 1) * tp - 1, att_cols:att_cols + rw_cols] for b in range(bp)])[None]
    shift_s = z_s[:, -1, att_cols:att_cols + rw_cols][None]

    tm_prep = 256
    prep_p = _rwkv_prep(z, 0, mp, att_cols, None, jnp.zeros((bp, 1, rwp), F32), tp, prm, tm=tm_prep)
    cols_s = z_s[:, :, att_cols:att_cols + rwp]
    shift_in = jnp.pad(state_shift[0], ((0, 0), (0, pad)))
    prev_s = jnp.concatenate([shift_in[:, None, :], cols_s[:, :-1]], axis=1).reshape(ms, rwp)
    prep_s = _rwkv_prep(z, mp, ms, att_cols, prev_s, None, ts, prm, tm=tm_prep)

    h_rw = c // RW_HEAD
    y_p, wkv_p = _wkv_scan(*prep_p[:6], jnp.zeros((1, bp, h_rw, RW_HEAD, RW_HEAD), F32), nb=bp, t=tp, tc=64)
    y_s, wkv_s = _wkv_scan(*prep_s[:6], state_wkv, nb=bs, t=ts, tc=ts)
    lw, lb = rw_lnx_w[0].reshape(1, c), rw_lnx_b[0].reshape(1, c)
    rw_p = _rw_post(y_p, prep_p[6], prep_p[7], lw, lb, seg_full, seg_full.T, tm=256)
    rw_s = _rw_post(y_s, prep_s[6], prep_s[7], lw, lb, seg_full, seg_full.T, tm=256)
    rw_act = jnp.concatenate([rw_p, rw_s], axis=0)

    n_heads_all = len(ATT_GROUPS) * ATT_HEADS
    slopes = 2.0 ** (-8.0 * jnp.arange(1, n_heads_all + 1, dtype=F32) / n_heads_all)
    att_p = _attn_prompt(z, slopes, nb=bp, t=tp)
    qs = [z_s[:, :, g * 3 * att_w:g * 3 * att_w + att_w].reshape(bs, ts, ATT_HEADS, ATT_HEAD_DIM)
          for g in range(len(ATT_GROUPS))]
    slope_tiles = jnp.broadcast_to(slopes.reshape(len(ATT_GROUPS), ATT_HEADS, 1),
                                   (len(ATT_GROUPS), ATT_HEADS, ATT_HEAD_DIM))
    att_s = _attn_sample(qs, [kv[0] for kv in new_kv_s], [cache_kv_0[0], cache_kv_1[0], cache_kv_2[0]], slope_tiles)
    att_act = jnp.concatenate([att_p, att_s.reshape(ms, att_w).astype(BF16)], axis=0)

    mixed = _mix(rw_act, att_act, rw_out[0].astype(BF16), att_out[0].astype(BF16), zg, 0, tm=1024, tn=512)
    x1 = _matmul(mixed, w_o[0].astype(BF16), tm=1024, tn=512, residual=x, name="w_o")
    hf = _rmsnorm(x1, norm_ffn[0], BF16)
    act = _swiglu(hf, w_ffn_in[0], tm=1024, tn=256)
    x2 = _matmul(act, w_ffn_out[0].astype(BF16), tm=512, tn=256, residual=x1, vmem_mib=56, name="w_ffn_out")
    y_p = _rmsnorm(x2, norm_final, F32, 0, mp)
    y_s = _rmsnorm(x2, norm_final, F32, mp, ms)

    return (y_p.reshape(bp, tp, d), y_s.reshape(bs, ts, d),
            new_kv_p[0], new_kv_p[1], new_kv_p[2], wkv_p, shift_p,
            new_kv_s[0], new_kv_s[1], new_kv_s[2], wkv_s, shift_s)
```

```python
import functools

import jax
import jax.numpy as jnp
from jax import lax
from jax.experimental import pallas as pl
from jax.experimental.pallas import tpu as pltpu

F32 = jnp.float32
BF16 = jnp.bfloat16

RMS_EPS = 1e-6
GN_EPS = 64e-5
RW_HEAD = 64
ATT_GROUPS = ((128, 1), (512, 4), (2048, 16))
ATT_HEAD_DIM = 128
ATT_HEADS = 8
LANES = 128
SUBLANES = 8
NEG_INF = float("-inf")


def _params(sem, vmem_mib):
    return pltpu.CompilerParams(dimension_semantics=sem, vmem_limit_bytes=vmem_mib << 20)


def _rmsnorm_kernel(x_ref, g_ref, o_ref):
    x = x_ref[...]
    ms = jnp.mean(x * x, axis=-1, keepdims=True)
    o_ref[...] = (x * lax.rsqrt(ms + RMS_EPS) * g_ref[...]).astype(o_ref.dtype)


def _rmsnorm(x, g, out_dtype, row0=0, rows=None, tm=256):
    d = x.shape[1]
    rows = x.shape[0] if rows is None else rows
    rb0 = row0 // tm
    return pl.pallas_call(
        _rmsnorm_kernel,
        grid=(rows // tm,),
        in_specs=[pl.BlockSpec((tm, d), lambda i: (rb0 + i, 0)), pl.BlockSpec((1, d), lambda i: (0, 0))],
        out_specs=pl.BlockSpec((tm, d), lambda i: (i, 0)),
        out_shape=jax.ShapeDtypeStruct((rows, d), out_dtype),
        compiler_params=_params(("parallel",), 32),
        name="rmsnorm",
    )(x, g.reshape(1, d))


def _mm_kernel(a_ref, w_ref, o_ref):
    o_ref[...] = jnp.dot(a_ref[...], w_ref[...].astype(BF16), preferred_element_type=F32).astype(o_ref.dtype)


def _mm_res_kernel(a_ref, w_ref, r_ref, o_ref):
    o_ref[...] = r_ref[...] + jnp.dot(a_ref[...], w_ref[...].astype(BF16), preferred_element_type=F32)


def _mm_nt_kernel(a_ref, wt_ref, o_ref):
    o_ref[...] = lax.dot_general(a_ref[...], wt_ref[...].astype(BF16), (((1,), (1,)), ((), ())),
                                 preferred_element_type=F32)


def _matmul(a, w, *, tm, tn, n_out=None, residual=None, transposed=False, vmem_mib=48, name="matmul"):
    m, k = a.shape
    if transposed:
        assert residual is None
        n = w.shape[0] if n_out is None else n_out
        w_spec = pl.BlockSpec((tn, k), lambda i, j: (j, 0))
        kern = _mm_nt_kernel
    else:
        n = w.shape[1] if n_out is None else n_out
        w_spec = pl.BlockSpec((k, tn), lambda i, j: (0, j))
        kern = _mm_kernel
    in_specs = [pl.BlockSpec((tm, k), lambda i, j: (i, 0)), w_spec]
    args = [a, w]
    if residual is not None:
        in_specs.append(pl.BlockSpec((tm, tn), lambda i, j: (i, j)))
        args.append(residual)
        kern = _mm_res_kernel
    return pl.pallas_call(
        kern,
        grid=(m // tm, n // tn),
        in_specs=in_specs,
        out_specs=pl.BlockSpec((tm, tn), lambda i, j: (i, j)),
        out_shape=jax.ShapeDtypeStruct((m, n), F32),
        compiler_params=_params(("parallel", "arbitrary"), vmem_mib),
        name=name,
    )(*args)


def _swiglu_kernel(a_ref, wg_ref, wu_ref, o_ref):
    a = a_ref[...]
    g = jnp.dot(a, wg_ref[...].astype(BF16), preferred_element_type=F32)
    u = jnp.dot(a, wu_ref[...].astype(BF16), preferred_element_type=F32)
    o_ref[...] = (g * jax.nn.sigmoid(g) * u).astype(o_ref.dtype)


def _swiglu(a, w, *, tm, tn):
    m, k = a.shape
    f = w.shape[1] // 2
    nb = f // tn
    return pl.pallas_call(
        _swiglu_kernel,
        grid=(m // tm, nb),
        in_specs=[
            pl.BlockSpec((tm, k), lambda i, j: (i, 0)),
            pl.BlockSpec((k, tn), lambda i, j: (0, j)),
            pl.BlockSpec((k, tn), lambda i, j: (0, j + nb)),
        ],
        out_specs=pl.BlockSpec((tm, tn), lambda i, j: (i, j)),
        out_shape=jax.ShapeDtypeStruct((m, f), BF16),
        compiler_params=_params(("parallel", "arbitrary"), 56),
        name="swiglu",
    )(a, w, w)


def _mix_kernel(rw_ref, at_ref, wr_ref, wa_ref, ga_ref, gb_ref, o_ref):
    rw = jnp.dot(rw_ref[...], wr_ref[...], preferred_element_type=F32)
    at = jnp.dot(at_ref[...], wa_ref[...], preferred_element_type=F32)
    mixed = jax.nn.sigmoid(ga_ref[...]) * rw + jax.nn.sigmoid(gb_ref[...]) * at
    o_ref[...] = mixed.astype(o_ref.dtype)


def _mix(rw, at, w_rw, w_at, z, gate_col0, *, tm, tn):
    m, kr = rw.shape
    ka = at.shape[1]
    d = w_rw.shape[1]
    nb = d // tn
    g0 = gate_col0 // tn
    return pl.pallas_call(
        _mix_kernel,
        grid=(m // tm, nb),
        in_specs=[
            pl.BlockSpec((tm, kr), lambda i, j: (i, 0)),
            pl.BlockSpec((tm, ka), lambda i, j: (i, 0)),
            pl.BlockSpec((kr, tn), lambda i, j: (0, j)),
            pl.BlockSpec((ka, tn), lambda i, j: (0, j)),
            pl.BlockSpec((tm, tn), lambda i, j: (i, g0 + j)),
            pl.BlockSpec((tm, tn), lambda i, j: (i, g0 + nb + j)),
        ],
        out_specs=pl.BlockSpec((tm, tn), lambda i, j: (i, j)),
        out_shape=jax.ShapeDtypeStruct((m, d), BF16),
        compiler_params=_params(("parallel", "arbitrary"), 48),
        name="mix",
    )(rw, at, w_rw, w_at, z, z)


def _split_bf16(x):
    hi = x.astype(BF16)
    lo = (x - hi.astype(F32)).astype(BF16)
    return hi, lo


def _seg_sum_bcast(x, seg, seg_t):
    hi, lo = _split_bf16(x)
    s = jnp.dot(hi, seg, preferred_element_type=F32) + jnp.dot(lo, seg, preferred_element_type=F32)
    shi, slo = _split_bf16(s)
    return jnp.dot(shi, seg_t, preferred_element_type=F32) + jnp.dot(slo, seg_t, preferred_element_type=F32)


def _prep_body(cur, prev, mu, w0, wup, a0, aup, gup, k_k, k_a, r_k, seg, seg_t, outs):
    r_ref, w_ref, k_ref, v_ref, kkn_ref, b_ref, g_ref, bonus_ref = outs
    xs = {n: cur[n] + (prev[n] - cur[n]) * mu[n] for n in ("r", "k", "v", "t")}
    r, k, v, t = xs["r"], xs["k"], xs["v"], xs["t"]
    wl = w0 + jnp.dot(jnp.tanh(t).astype(BF16), wup, preferred_element_type=F32)
    w_log = -(jnp.maximum(-wl, 0.0) + jnp.log(1.0 + jnp.exp(-jnp.abs(wl)))) - 0.5
    decay = jnp.exp(-jnp.exp(w_log))
    a = jax.nn.sigmoid(a0 + jnp.dot(t.astype(BF16), aup, preferred_element_type=F32))
    g = jnp.dot(jax.nn.sigmoid(t).astype(BF16), gup, preferred_element_type=F32)
    kk = k * k_k
    kk = kk * lax.rsqrt(jnp.maximum(_seg_sum_bcast(kk * kk, seg, seg_t), 1e-24))
    k_mod = k * (1.0 + (a - 1.0) * k_a)
    bonus = _seg_sum_bcast(r * k_mod * r_k, seg, seg_t) * v
    r_ref[...] = r
    w_ref[...] = decay
    k_ref[...] = k_mod
    v_ref[...] = v
    kkn_ref[...] = -kk
    b_ref[...] = kk * a
    g_ref[...] = g
    bonus_ref[...] = bonus


def _prep_roll_kernel(zr, zk, zv, zt, br, bk, bv, bt, sr, sk, sv, st, mr, mk, mv, mt, w0, wup, a0, aup, gup, k_k,
                      k_a, r_k, seg, seg_t, *outs, tiles_per_seq):
    cur = {"r": zr[...], "k": zk[...], "v": zv[...], "t": zt[...]}
    seq_start = pl.program_id(0) % tiles_per_seq == 0
    prev = {}
    for n, bref, sref in (("r", br, sr), ("k", bk, sk), ("v", bv, sv), ("t", bt, st)):
        x = cur[n]
        rolled = pltpu.roll(x, 1, 0)
        row0 = lax.broadcasted_iota(jnp.int32, x.shape, 0) == 0
        first = jnp.where(seq_start, sref[...], bref[SUBLANES - 1:, :])
        prev[n] = jnp.where(row0, first, rolled)
    mu = {"r": mr[...], "k": mk[...], "v": mv[...], "t": mt[...]}
    _prep_body(cur, prev, mu, w0[...], wup[...], a0[...], aup[...], gup[...], k_k[...], k_a[...], r_k[...],
               seg[...], seg_t[...], outs)


def _prep_given_kernel(zr, zk, zv, zt, pr, pk, pv, pt, mr, mk, mv, mt, w0, wup, a0, aup, gup, k_k, k_a, r_k, seg,
                       seg_t, *outs):
    cur = {"r": zr[...], "k": zk[...], "v": zv[...], "t": zt[...]}
    prev = {"r": pr[...], "k": pk[...], "v": pv[...], "t": pt[...]}
    mu = {"r": mr[...], "k": mk[...], "v": mv[...], "t": mt[...]}
    _prep_body(cur, prev, mu, w0[...], wup[...], a0[...], aup[...], gup[...], k_k[...], k_a[...], r_k[...],
               seg[...], seg_t[...], outs)


def _rwkv_prep(z, row0, rows, rw_col0, prev_arr, shift_arr, seq_len, p, *, tm):
    c = p["w0"].shape[1]
    hc = c // 2
    nrb = rows // tm
    rb0 = row0 // tm
    cb = rw_col0 // hc
    tb = (rw_col0 + 3 * c) // 512
    zspec = lambda off: pl.BlockSpec((tm, hc), lambda i, j: (rb0 + i, cb + off + j))
    in_specs = [zspec(0), zspec(2), zspec(4), pl.BlockSpec((tm, 512), lambda i, j: (rb0 + i, tb))]
    args = [z, z, z, z]
    if prev_arr is not None:
        kern = _prep_given_kernel
        pspec = lambda off: pl.BlockSpec((tm, hc), lambda i, j: (i, off + j))
        in_specs += [pspec(0), pspec(2), pspec(4), pl.BlockSpec((tm, 512), lambda i, j: (i, 3 * c // 512))]
        args += [prev_arr] * 4
    else:
        tiles_per_seq = seq_len // tm
        kern = functools.partial(_prep_roll_kernel, tiles_per_seq=tiles_per_seq)
        before = lambda i: jnp.maximum((rb0 + i) * (tm // SUBLANES) - 1, 0)
        bspec = lambda off: pl.BlockSpec((SUBLANES, hc), lambda i, j: (before(i), cb + off + j))
        in_specs += [bspec(0), bspec(2), bspec(4), pl.BlockSpec((SUBLANES, 512), lambda i, j: (before(i), tb))]
        args += [z] * 4
        sspec = lambda off: pl.BlockSpec((None, 1, hc), lambda i, j: (i // tiles_per_seq, 0, off + j))
        in_specs += [sspec(0), sspec(2), sspec(4),
                     pl.BlockSpec((None, 1, 512), lambda i, j: (i // tiles_per_seq, 0, 3 * c // 512))]
        args += [shift_arr] * 4
    mspec = lambda off: pl.BlockSpec((1, hc), lambda i, j: (0, off + j))
    in_specs += [mspec(0), mspec(2), mspec(4), pl.BlockSpec((1, 512), lambda i, j: (0, 3 * c // 512))]
    args += [p["mu"]] * 4
    vec = pl.BlockSpec((1, hc), lambda i, j: (0, j))
    lowrank = pl.BlockSpec((512, hc), lambda i, j: (0, j))
    in_specs += [vec, lowrank, vec, lowrank, lowrank, vec, vec, vec,
                 pl.BlockSpec((hc, LANES), lambda i, j: (0, 0)), pl.BlockSpec((LANES, hc), lambda i, j: (0, 0))]
    args += [p["w0"], p["wup"], p["a0"], p["aup"], p["gup"], p["k_k"], p["k_a"], p["r_k"], p["seg"], p["seg_t"]]
    out_spec = pl.BlockSpec((tm, hc), lambda i, j: (i, j))
    return pl.pallas_call(
        kern,
        grid=(nrb, 2),
        in_specs=in_specs,
        out_specs=[out_spec] * 8,
        out_shape=[jax.ShapeDtypeStruct((rows, c), F32)] * 8,
        compiler_params=_params(("parallel", "arbitrary"), 48),
        name="rwkv_prep",
    )(*args)


N_PAIRS = 16
N_V_MXU = 0


def _trunc_bf16(x):
    return lax.bitcast_convert_type(lax.bitcast_convert_type(x, jnp.int32) & jnp.int32(-65536), F32)


def _wkv_kernel(r_ref, w_ref, k_ref, v_ref, kkn_ref, b_ref, s0_ref, y_ref, st_ref, s_scr, y_scr, *, tc):
    ci = pl.program_id(1)

    @pl.when(ci == 0)
    def _():
        for p in range(N_PAIRS):
            s_scr[p] = jnp.concatenate([s0_ref[2 * p], s0_ref[2 * p + 1]], axis=1)

    y_scr[...] = jnp.zeros(y_scr.shape, F32)
    lane = lax.broadcasted_iota(jnp.int32, (RW_HEAD, LANES), 1)
    sub = lax.broadcasted_iota(jnp.int32, (RW_HEAD, LANES), 0)
    head_a = lane < RW_HEAD
    diag_a = lane == sub
    diag_b = lane == sub + RW_HEAD
    diag = (lane & (RW_HEAD - 1)) == sub
    n_v = N_V_MXU * RW_HEAD
    wr = lax.broadcasted_iota(jnp.int32, (2 * LANES, 2 * LANES), 0)
    wc = lax.broadcasted_iota(jnp.int32, (2 * LANES, 2 * LANES), 1)
    ones_bd = ((wr // RW_HEAD) == (wc // RW_HEAD)).astype(F32)
    n_sa = N_PAIRS * RW_HEAD

    def readout(res, ymask):
        for p in range(N_PAIRS):
            blk = n_sa + RW_HEAD * (p // 2)
            yb = res[blk:blk + RW_HEAD, LANES * (p % 2):LANES * (p % 2 + 1)]
            y_scr[p] = jnp.where(ymask, yb, y_scr[p])

    def step_group(gi, carry):
        t0 = pl.multiple_of(gi * SUBLANES, SUBLANES)
        tp0 = pl.multiple_of(jnp.maximum(gi - 1, 0) * SUBLANES, SUBLANES)
        rows8 = lambda ref, p: ref[pl.ds(t0, SUBLANES), LANES * p:LANES * (p + 1)]
        kkn8 = [rows8(kkn_ref, p) for p in range(N_PAIRS)]
        r8 = [rows8(r_ref, p) for p in range(N_PAIRS)]
        v8 = [rows8(v_ref, p) for p in range(N_PAIRS)]
        w8 = [rows8(w_ref, p) for p in range(N_PAIRS)]
        b8 = [rows8(b_ref, p) for p in range(N_PAIRS)]
        k8 = [rows8(k_ref, p) for p in range(N_PAIRS)]
        r_last = [r_ref[pl.ds(tp0, SUBLANES), LANES * p:LANES * (p + 1)][SUBLANES - 1:] for p in range(N_PAIRS)]
        for u in range(SUBLANES):
            ymask = (lane & (RW_HEAD - 1)) == (t0 + (u - 1))
            lhs_sa, lhs_y, lhs_v, qs = [], [], [], []
            for p in range(N_PAIRS):
                s_p = s_scr[p]
                pr = s_p * kkn8[p][u:u + 1]
                pr_hi = _trunc_bf16(pr)
                lhs_sa.append(jnp.concatenate([pr_hi, pr - pr_hi], axis=1))
                qs.append(s_p * (r8[p][u - 1:u] if u else r_last[p]))
                if p % 2:
                    lhs_y.append(jnp.concatenate(qs[-2:], axis=1))
                if p < N_V_MXU:
                    vrow = v8[p][u:u + 1]
                    v_hi = _trunc_bf16(vrow)
                    lhs_v.append(jnp.concatenate(
                        [jnp.where(diag, v_hi, 0.0), jnp.where(diag, vrow - v_hi, 0.0)], axis=1))
            res = jnp.dot(jnp.concatenate(lhs_v + lhs_sa + lhs_y, axis=0), ones_bd, preferred_element_type=F32)
            res_v, res = res[:n_v], res[n_v:]
            for p in range(N_PAIRS):
                sa2 = res[RW_HEAD * p:RW_HEAD * (p + 1)]
                sa = sa2[:, :LANES] + sa2[:, LANES:]
                if p < N_V_MXU:
                    v2 = res_v[RW_HEAD * p:RW_HEAD * (p + 1)]
                    vcol = v2[:, :LANES] + v2[:, LANES:]
                else:
                    vrow = v8[p][u:u + 1]
                    va = jnp.sum(jnp.where(diag_a, vrow, 0.0), axis=1, keepdims=True)
                    vb = jnp.sum(jnp.where(diag_b, vrow, 0.0), axis=1, keepdims=True)
                    vcol = jnp.where(head_a, va, vb)
                s_scr[p] = s_scr[p] * w8[p][u:u + 1] + sa * b8[p][u:u + 1] + vcol * k8[p][u:u + 1]
            readout(res, ymask)
        return carry

    lax.fori_loop(0, tc // SUBLANES, step_group, 0)

    qs = [s_scr[p] * r_ref[pl.ds(tc - 1, 1), slice(LANES * p, LANES * (p + 1))] for p in range(N_PAIRS)]
    lhs_y = [jnp.concatenate(qs[p:p + 2], axis=1) for p in range(0, N_PAIRS, 2)]
    res = jnp.dot(jnp.concatenate(lhs_y, axis=0), ones_bd, preferred_element_type=F32)
    readout(jnp.concatenate([jnp.zeros((n_sa, 2 * LANES), F32), res], axis=0), (lane & (RW_HEAD - 1)) == (tc - 1))
    for p in range(N_PAIRS):
        yt = y_scr[p].T
        rows = jnp.concatenate([yt[:RW_HEAD], yt[RW_HEAD:]], axis=1)
        y_ref[:, LANES * p:LANES * (p + 1)] = rows[:tc]

    @pl.when(ci == pl.num_programs(1) - 1)
    def _():
        for p in range(N_PAIRS):
            s_p = s_scr[p]
            st_ref[2 * p] = s_p[:, :RW_HEAD]
            st_ref[2 * p + 1] = s_p[:, RW_HEAD:]


def _wkv_scan(r, w, k, v, kkn, b, s0, *, nb, t, tc):
    c = r.shape[1]
    h = c // RW_HEAD
    nc = t // tc
    row = pl.BlockSpec((tc, c), lambda bi, ci: (bi * nc + ci, 0))
    st = pl.BlockSpec((None, None, h, RW_HEAD, RW_HEAD), lambda bi, ci: (0, bi, 0, 0, 0))
    return pl.pallas_call(
        functools.partial(_wkv_kernel, tc=tc),
        grid=(nb, nc),
        in_specs=[row] * 6 + [st],
        out_specs=[row, st],
        out_shape=[jax.ShapeDtypeStruct((nb * t, c), F32), jax.ShapeDtypeStruct((1, nb, h, RW_HEAD, RW_HEAD), F32)],
        scratch_shapes=[pltpu.VMEM((N_PAIRS, RW_HEAD, LANES), F32), pltpu.VMEM((N_PAIRS, RW_HEAD, LANES), F32)],
        compiler_params=_params(("parallel", "arbitrary"), 48),
        name="wkv_scan",
    )(r, w, k, v, kkn, b, s0)


def _wkv_lanes_kernel(r_ref, w_ref, k_ref, v_ref, kkn_ref, b_ref, s0_ref, y_ref, st_ref, *, steps):
    def value_group(ig, carry):
        i0 = pl.multiple_of(ig * SUBLANES, SUBLANES)
        v8 = [v_ref[s, pl.ds(i0, SUBLANES), :] for s in range(steps)]
        ys = [[] for _ in range(steps)]
        for ii in range(SUBLANES):
            s_i = s0_ref[i0 + ii]
            for s in range(steps):
                sa = jnp.sum(s_i * kkn_ref[s], axis=0, keepdims=True)
                s_i = s_i * w_ref[s] + sa * b_ref[s] + v8[s][ii:ii + 1] * k_ref[s]
                ys[s].append(jnp.sum(s_i * r_ref[s], axis=0, keepdims=True))
            st_ref[i0 + ii] = s_i
        for s in range(steps):
            y_ref[s, pl.ds(i0, SUBLANES), :] = jnp.concatenate(ys[s], axis=0)
        return carry

    lax.fori_loop(0, RW_HEAD // SUBLANES, value_group, 0)


def _wkv_scan_lanes(vecs, s0, *, nb, t):
    c = vecs[0].shape[1]
    h = c // RW_HEAD
    vec = pl.BlockSpec((t, RW_HEAD, nb), lambda hi: (0, hi, 0))
    st = pl.BlockSpec((None, RW_HEAD, RW_HEAD, nb), lambda hi: (hi, 0, 0, 0))
    return pl.pallas_call(
        functools.partial(_wkv_lanes_kernel, steps=t),
        grid=(h,),
        in_specs=[vec] * 6 + [st],
        out_specs=[vec, st],
        out_shape=[jax.ShapeDtypeStruct((t, c, nb), F32), jax.ShapeDtypeStruct((h, RW_HEAD, RW_HEAD, nb), F32)],
        compiler_params=_params(("parallel",), 48),
        name="wkv_scan_lanes",
    )(*vecs, s0)


def _rw_post_kernel(y_ref, g_ref, bonus_ref, lw_ref, lb_ref, seg_ref, segt_ref, o_ref):
    y = y_ref[...]
    seg = seg_ref[...]
    seg_t = segt_ref[...]
    inv = 1.0 / RW_HEAD
    mu = _seg_sum_bcast(y, seg, seg_t) * inv
    d = y - mu
    var = _seg_sum_bcast(d * d, seg, seg_t) * inv
    yn = d * lax.rsqrt(var + GN_EPS)
    o_ref[...] = ((yn * lw_ref[...] + lb_ref[...] + bonus_ref[...]) * g_ref[...]).astype(o_ref.dtype)


def _rw_post(y, g, bonus, lnx_w, lnx_b, seg, seg_t, *, tm):
    m, c = y.shape
    blk = pl.BlockSpec((tm, c), lambda i: (i, 0))
    vec = pl.BlockSpec((1, c), lambda i: (0, 0))
    return pl.pallas_call(
        _rw_post_kernel,
        grid=(m // tm,),
        in_specs=[blk, blk, blk, vec, vec, pl.BlockSpec((c, LANES), lambda i: (0, 0)),
                  pl.BlockSpec((LANES, c), lambda i: (0, 0))],
        out_specs=blk,
        out_shape=jax.ShapeDtypeStruct((m, c), BF16),
        compiler_params=_params(("parallel",), 48),
        name="rw_post",
    )(y, g, bonus, lnx_w, lnx_b, seg, seg_t)


QB = 128


def _attn_prompt_kernel(sl_ref, q0, k0, v0, q1, k1, v1, q2, k2, v2, out_ref, o_scr, l_scr, *, t):
    h = pl.program_id(1)
    scale = ATT_HEAD_DIM ** -0.5
    iq = lax.broadcasted_iota(jnp.int32, (QB, QB), 0)
    ik = lax.broadcasted_iota(jnp.int32, (QB, QB), 1)
    d_cur = (iq - ik).astype(F32)
    refs = ((q0, k0, v0), (q1, k1, v1), (q2, k2, v2))
    for g, (win, dil) in enumerate(ATT_GROUPS):
        q_ref, k_ref, v_ref = refs[g]
        slope = sl_ref[g * ATT_HEADS + h]
        nblk = t // dil // QB
        bias_cur = jnp.where(ik <= iq, -slope * dil * d_cur, NEG_INF)
        bias_prev = jnp.where(ik >= iq, -slope * dil * (d_cur + QB), NEG_INF)

        def block(idx, carry, q_ref=q_ref, k_ref=k_ref, v_ref=v_ref, dil=dil, nblk=nblk, g=g,
                  bias_cur=bias_cur, bias_prev=bias_prev):
            res = idx // nblk
            qb = idx % nblk
            base = res + qb * (QB * dil)
            pbase = res + jnp.maximum(qb - 1, 0) * (QB * dil)
            rows = pl.ds(base, QB, stride=dil) if dil > 1 else pl.ds(base, QB)
            prows = pl.ds(pbase, QB, stride=dil) if dil > 1 else pl.ds(pbase, QB)
            q = q_ref[rows, :].astype(BF16)
            kc = k_ref[rows, :].astype(BF16)
            dims = (((1,), (1,)), ((), ()))
            s_c = lax.dot_general(q, kc, dims, preferred_element_type=F32) * scale + bias_cur
            m = jnp.max(s_c, axis=-1, keepdims=True)
            if nblk > 1:
                kp = k_ref[prows, :].astype(BF16)
                s_p = lax.dot_general(q, kp, dims, preferred_element_type=F32) * scale + jnp.where(
                    qb > 0, bias_prev, NEG_INF)
                m = jnp.maximum(m, jnp.max(s_p, axis=-1, keepdims=True))
            p_c = jnp.exp(s_c - m)
            l = jnp.sum(p_c, axis=-1, keepdims=True)
            o = jnp.dot(p_c.astype(BF16), v_ref[rows, :].astype(BF16), preferred_element_type=F32)
            if nblk > 1:
                p_p = jnp.exp(s_p - m)
                l = l + jnp.sum(p_p, axis=-1, keepdims=True)
                o = o + jnp.dot(p_p.astype(BF16), v_ref[prows, :].astype(BF16), preferred_element_type=F32)
            o_scr[g, rows, :] = o / l
            l_scr[g, rows, :] = jnp.broadcast_to(m + jnp.log(l), (QB, ATT_HEAD_DIM))
            return carry

        lax.fori_loop(0, dil * nblk, block, 0, unroll=4)

    l0, l1, l2 = l_scr[0], l_scr[1], l_scr[2]
    m = jnp.maximum(jnp.maximum(l0, l1), l2)
    e0, e1, e2 = jnp.exp(l0 - m), jnp.exp(l1 - m), jnp.exp(l2 - m)
    den = e0 + e1 + e2
    out_ref[...] = ((e0 * o_scr[0] + e1 * o_scr[1] + e2 * o_scr[2]) / den).astype(out_ref.dtype)


def _attn_prompt(z, slopes, *, nb, t):
    specs = []
    for g in range(len(ATT_GROUPS)):
        for c in range(3):
            cb = (g * 3 + c) * ATT_HEADS
            specs.append(pl.BlockSpec((t, ATT_HEAD_DIM), lambda b, h, sl, cb=cb: (b, cb + h)))
    grid_spec = pltpu.PrefetchScalarGridSpec(
        num_scalar_prefetch=1,
        grid=(nb, ATT_HEADS),
        in_specs=specs,
        out_specs=pl.BlockSpec((t, ATT_HEAD_DIM), lambda b, h, sl: (b, h)),
        scratch_shapes=[pltpu.VMEM((3, t, ATT_HEAD_DIM), F32), pltpu.VMEM((3, t, ATT_HEAD_DIM), F32)],
    )
    return pl.pallas_call(
        functools.partial(_attn_prompt_kernel, t=t),
        grid_spec=grid_spec,
        out_shape=jax.ShapeDtypeStruct((nb * t, ATT_HEADS * ATT_HEAD_DIM), BF16),
        compiler_params=_params(("parallel", "arbitrary"), 48),
        name="attn_prompt",
    )(slopes, *([z] * 9))


N_NEW = 8
N_BAND = 128


def _attn_sample_kernel(q0, q1, q2, n0, n1, n2, c0, c1, c2, sl_ref, out_ref, sc_scr, o_scr, l_scr):
    scale = ATT_HEAD_DIM ** -0.5
    qs, news, caches = (q0, q1, q2), (n0, n1, n2), (c0, c1, c2)
    tile = (ATT_HEADS, ATT_HEAD_DIM)
    for g, (win, dil) in enumerate(ATT_GROUPS):
        q_ref, n_ref, c_ref = qs[g], news[g], caches[g]
        slope = sl_ref[g]
        past = N_BAND * dil
        qv = [q_ref[s] * scale for s in range(N_NEW)]
        if dil == 1:
            combos = [(0, s) for s in range(N_NEW)]
        elif dil == 4:
            combos = [(r, s) for r in range(4) for s in (r, r + 4)]
        else:
            combos = [(r, r) for r in range(N_NEW)]

        def row_tiles(m, r):
            if dil == 1:
                return c_ref[m, 0], c_ref[m, 1]
            if dil == 4:
                a = m * 4 + r
                return c_ref[a, 0], c_ref[a, 1]
            return c_ref[m, r, 0], c_ref[m, r, 1]

        def score(m, r, s):
            a = m * dil + r
            k_t, _ = row_tiles(m, r)
            sc = jnp.sum(qv[s] * k_t, axis=-1, keepdims=True) - slope * jnp.asarray(past + s - a, F32)
            if dil == 1:
                sc = jnp.where(m >= s, sc, NEG_INF)
            elif dil == 4 and s >= 4:
                sc = jnp.where(m >= 1, sc, NEG_INF)
            return jnp.broadcast_to(sc, tile)

        new_sc = []
        for s in range(N_NEW):
            lst = []
            for j in range(s // dil + 1):
                s2 = s - j * dil
                sc = jnp.sum(qv[s] * n_ref[s2, 0], axis=-1, keepdims=True) - slope * float(j * dil)
                lst.append((s2, jnp.broadcast_to(sc, tile)))
            new_sc.append(lst)
        m0 = [functools.reduce(jnp.maximum, [x for _, x in new_sc[s]]) for s in range(N_NEW)]

        def pass1(m, mx):
            mx = list(mx)
            for ci, (r, s) in enumerate(combos):
                sc = score(m, r, s)
                sc_scr[m, ci] = sc
                mx[s] = jnp.maximum(mx[s], sc)
            return tuple(mx)

        mx = lax.fori_loop(0, N_BAND, pass1, tuple(m0), unroll=32)

        l_init, o_init = [], []
        for s in range(N_NEW):
            l_s = jnp.zeros(tile, F32)
            o_s = jnp.zeros(tile, F32)
            for s2, sc in new_sc[s]:
                p = jnp.exp(sc - mx[s])
                l_s = l_s + p
                o_s = o_s + p * n_ref[s2, 1]
            l_init.append(l_s)
            o_init.append(o_s)

        def pass2(m, carry):
            ls, os_ = list(carry[0]), list(carry[1])
            for ci, (r, s) in enumerate(combos):
                _, v_t = row_tiles(m, r)
                p = jnp.exp(sc_scr[m, ci] - mx[s])
                ls[s] = ls[s] + p
                os_[s] = os_[s] + p * v_t
            return tuple(ls), tuple(os_)

        ls, os_ = lax.fori_loop(0, N_BAND, pass2, (tuple(l_init), tuple(o_init)), unroll=8)
        for s in range(N_NEW):
            o_scr[g, s] = os_[s] / ls[s]
            l_scr[g, s] = mx[s] + jnp.log(ls[s])

    for s in range(N_NEW):
        l0, l1, l2 = l_scr[0, s], l_scr[1, s], l_scr[2, s]
        m = jnp.maximum(jnp.maximum(l0, l1), l2)
        e0, e1, e2 = jnp.exp(l0 - m), jnp.exp(l1 - m), jnp.exp(l2 - m)
        out_ref[s] = (e0 * o_scr[0, s] + e1 * o_scr[1, s] + e2 * o_scr[2, s]) / (e0 + e1 + e2)


def _attn_sample(qs, news, caches, slope_tiles):
    nb = qs[0].shape[0]
    he = (ATT_HEADS, ATT_HEAD_DIM)
    qspec = pl.BlockSpec((None, N_NEW) + he, lambda b: (b, 0, 0, 0))
    nspec = pl.BlockSpec((None, N_NEW, 2) + he, lambda b: (b, 0, 0, 0, 0))
    c2 = caches[2]
    dil2 = ATT_GROUPS[2][1]
    c2v = c2.reshape(nb, c2.shape[1] // dil2, dil2, 2, *he)
    cspecs = [
        pl.BlockSpec((None, caches[0].shape[1], 2) + he, lambda b: (b, 0, 0, 0, 0)),
        pl.BlockSpec((None, caches[1].shape[1], 2) + he, lambda b: (b, 0, 0, 0, 0)),
        pl.BlockSpec((None, N_BAND, N_NEW, 2) + he, lambda b: (b, 0, 0, 0, 0, 0)),
    ]
    return pl.pallas_call(
        _attn_sample_kernel,
        grid=(nb,),
        in_specs=[qspec] * 3 + [nspec] * 3 + cspecs + [pl.BlockSpec((3,) + he, lambda b: (0, 0, 0))],
        out_specs=pl.BlockSpec((None, N_NEW) + he, lambda b: (b, 0, 0, 0)),
        out_shape=jax.ShapeDtypeStruct((nb, N_NEW) + he, F32),
        scratch_shapes=[pltpu.VMEM((N_BAND, N_NEW) + he, F32), pltpu.VMEM((3, N_NEW) + he, F32),
                        pltpu.VMEM((3, N_NEW) + he, F32)],
        compiler_params=_params(("parallel",), 48),
        name="attn_sample",
    )(*qs, *news, caches[0], caches[1], c2v, slope_tiles)


def kernel(x_prompt, x_sample, cache_kv_0, cache_kv_1, cache_kv_2, state_wkv, state_shift, norm_mix, w_in, rw_mu,
           rw_w0, rw_w_up, rw_a0, rw_a_up, rw_g_up, rw_k_k, rw_k_a, rw_r_k, rw_lnx_w, rw_lnx_b, rw_out, att_out,
           w_o, norm_ffn, w_ffn_in, w_ffn_out, norm_final):
    depth = w_in.shape[0]
    assert depth == 1, "single-layer stack"
    bp, tp, d = x_prompt.shape
    bs, ts, _ = x_sample.shape
    mp, ms = bp * tp, bs * ts
    assert ts == N_NEW
    c = rw_w0.shape[1]
    r_dec, r_aaa, r_gate = rw_w_up.shape[1], rw_a_up.shape[1], rw_g_up.shape[1]
    rw_cols = 3 * c + r_dec + r_aaa + r_gate
    att_cols = len(ATT_GROUPS) * 3 * ATT_HEADS * ATT_HEAD_DIM
    att_w = ATT_HEADS * ATT_HEAD_DIM
    pad = (-(att_cols + rw_cols)) % LANES
    rwp = rw_cols + pad
    assert rwp - 3 * c == 512
    nz = att_cols + rwp

    tail = lambda w, off: jnp.zeros((512, c), F32).at[off:off + w.shape[0]].set(w).astype(BF16)
    row = lambda v, n: jnp.pad(v.reshape(1, -1), ((0, 0), (0, n - v.size)))
    seg = (jnp.arange(c // 2)[:, None] // RW_HEAD == jnp.arange(LANES)[None, :]).astype(BF16)
    prm = {
        "mu": row(rw_mu[0], rwp), "w0": row(rw_w0[0], c), "a0": row(rw_a0[0], c),
        "wup": tail(rw_w_up[0], 0), "aup": tail(rw_a_up[0], r_dec), "gup": tail(rw_g_up[0], r_dec + r_aaa),
        "k_k": row(rw_k_k[0], c), "k_a": row(rw_k_a[0], c), "r_k": row(rw_r_k[0].reshape(-1), c),
        "seg": seg, "seg_t": seg.T,
    }
    seg_full = (jnp.arange(c)[:, None] // RW_HEAD == jnp.arange(LANES)[None, :]).astype(BF16)

    x = jnp.concatenate([x_prompt.reshape(mp, d), x_sample.reshape(ms, d)], axis=0)
    m_all = mp + ms
    hmix = _rmsnorm(x, norm_mix[0], BF16)
    wi_t = jnp.swapaxes(w_in, 1, 2)[0]
    z = _matmul(hmix, wi_t, tm=1024, tn=512, n_out=nz, transposed=True, vmem_mib=56, name="w_in")
    zg = _matmul(hmix, wi_t[att_cols + rw_cols:], tm=1024, tn=512, transposed=True, vmem_mib=56,
                 name="w_gate")
    z_s = z[mp:].reshape(bs, ts, nz)

    new_kv_p, new_kv_s = [], []
    for g, (win, dil) in enumerate(ATT_GROUPS):
        lo = g * 3 * att_w + att_w
        keep = min(win, tp)
        kv = jnp.stack([z[(b + 1) * tp - keep:(b + 1) * tp, lo:lo + 2 * att_w] for b in range(bp)])
        new_kv_p.append(kv.reshape(bp, keep, 2, ATT_HEADS, ATT_HEAD_DIM)[None])
        new_kv_s.append(z_s[:, :, lo:lo + 2 * att_w].reshape(bs, ts, 2, ATT_HEADS, ATT_HEAD_DIM)[None])
    shift_p = jnp.stack([z[(b + 1) * tp - 1, att_cols:att_cols + rw_cols] for b in range(bp)])[None]
    shift_s = z_s[:, -1, att_cols:att_cols + rw_cols][None]

    tm_prep = 256
    prep_p = _rwkv_prep(z, 0, mp, att_cols, None, jnp.zeros((bp, 1, rwp), F32), tp, prm, tm=tm_prep)
    cols_s = z_s[:, :, att_cols:att_cols + rwp]
    shift_in = jnp.pad(state_shift[0], ((0, 0), (0, pad)))
    prev_s = jnp.concatenate([shift_in[:, None, :], cols_s[:, :-1]], axis=1).reshape(ms, rwp)
    prep_s = _rwkv_prep(z, mp, ms, att_cols, prev_s, None, ts, prm, tm=tm_prep)

    h_rw = c // RW_HEAD
    y_p, wkv_p = _wkv_scan(*prep_p[:6], jnp.zeros((1, bp, h_rw, RW_HEAD, RW_HEAD), F32), nb=bp, t=tp, tc=64)
    to_lanes = lambda a: a.reshape(bs, ts, c).transpose(1, 2, 0)
    y_t, wkv_t = _wkv_scan_lanes([to_lanes(a) for a in prep_s[:6]], jnp.transpose(state_wkv[0], (1, 2, 3, 0)),
                                 nb=bs, t=ts)
    y_s = y_t.transpose(2, 0, 1).reshape(ms, c)
    wkv_s = jnp.transpose(wkv_t, (3, 0, 1, 2))[None]
    lw, lb = rw_lnx_w[0].reshape(1, c), rw_lnx_b[0].reshape(1, c)
    rw_p = _rw_post(y_p, prep_p[6], prep_p[7], lw, lb, seg_full, seg_full.T, tm=256)
    rw_s = _rw_post(y_s, prep_s[6], prep_s[7], lw, lb, seg_full, seg_full.T, tm=256)
    rw_act = jnp.concatenate([rw_p, rw_s], axis=0)

    n_heads_all = len(ATT_GROUPS) * ATT_HEADS
    slopes = 2.0 ** (-8.0 * jnp.arange(1, n_heads_all + 1, dtype=F32) / n_heads_all)
    att_p = _attn_prompt(z, slopes, nb=bp, t=tp)
    qs = [z_s[:, :, g * 3 * att_w:g * 3 * att_w + att_w].reshape(bs, ts, ATT_HEADS, ATT_HEAD_DIM)
          for g in range(len(ATT_GROUPS))]
    slope_tiles = jnp.broadcast_to(slopes.reshape(len(ATT_GROUPS), ATT_HEADS, 1),
                                   (len(ATT_GROUPS), ATT_HEADS, ATT_HEAD_DIM))
    att_s = _attn_sample(qs, [kv[0] for kv in new_kv_s], [cache_kv_0[0], cache_kv_1[0], cache_kv_2[0]], slope_tiles)
    att_act = jnp.concatenate([att_p, att_s.reshape(ms, att_w).astype(BF16)], axis=0)

    mixed = _mix(rw_act, att_act, rw_out[0].astype(BF16), att_out[0].astype(BF16), zg, 0, tm=1024, tn=512)
    x1 = _matmul(mixed, w_o[0].astype(BF16), tm=1024, tn=512, residual=x, name="w_o")
    hf = _rmsnorm(x1, norm_ffn[0], BF16)
    act = _swiglu(hf, w_ffn_in[0], tm=1024, tn=256)
    x2 = _matmul(act, w_ffn_out[0].astype(BF16), tm=512, tn=256, residual=x1, vmem_mib=56, name="w_ffn_out")
    y_p = _rmsnorm(x2, norm_final, F32, 0, mp)
    y_s = _rmsnorm(x2, norm_final, F32, mp, ms)

    return (y_p.reshape(bp, tp, d), y_s.reshape(bs, ts, d),
            new_kv_p[0], new_kv_p[1], new_kv_p[2], wkv_p, shift_p,
            new_kv_s[0], new_kv_s[1], new_kv_s[2], wkv_s, shift_s)
```

```python
import functools

import jax
import jax.numpy as jnp
from jax import lax
from jax.experimental import pallas as pl
from jax.experimental.pallas import tpu as pltpu

F32 = jnp.float32
BF16 = jnp.bfloat16

RMS_EPS = 1e-6
GN_EPS = 64e-5
RW_HEAD = 64
ATT_GROUPS = ((128, 1), (512, 4), (2048, 16))
ATT_HEAD_DIM = 128
ATT_HEADS = 8
LANES = 128
SUBLANES = 8
NEG_INF = float("-inf")


def _params(sem, vmem_mib):
    return pltpu.CompilerParams(dimension_semantics=sem, vmem_limit_bytes=vmem_mib << 20)


def _rmsnorm_kernel(x_ref, g_ref, o_ref):
    x = x_ref[...]
    ms = jnp.mean(x * x, axis=-1, keepdims=True)
    o_ref[...] = (x * lax.rsqrt(ms + RMS_EPS) * g_ref[...]).astype(o_ref.dtype)


def _rmsnorm(x, g, out_dtype, row0=0, rows=None, tm=256):
    d = x.shape[1]
    rows = x.shape[0] if rows is None else rows
    rb0 = row0 // tm
    return pl.pallas_call(
        _rmsnorm_kernel,
        grid=(rows // tm,),
        in_specs=[pl.BlockSpec((tm, d), lambda i: (rb0 + i, 0)), pl.BlockSpec((1, d), lambda i: (0, 0))],
        out_specs=pl.BlockSpec((tm, d), lambda i: (i, 0)),
        out_shape=jax.ShapeDtypeStruct((rows, d), out_dtype),
        compiler_params=_params(("parallel",), 32),
        name="rmsnorm",
    )(x, g.reshape(1, d))


def _mm_kernel(a_ref, w_ref, o_ref):
    o_ref[...] = jnp.dot(a_ref[...], w_ref[...].astype(BF16), preferred_element_type=F32).astype(o_ref.dtype)


def _mm_res_kernel(a_ref, w_ref, r_ref, o_ref):
    o_ref[...] = r_ref[...] + jnp.dot(a_ref[...], w_ref[...].astype(BF16), preferred_element_type=F32)


def _mm_nt_kernel(a_ref, wt_ref, o_ref):
    o_ref[...] = lax.dot_general(a_ref[...], wt_ref[...].astype(BF16), (((1,), (1,)), ((), ())),
                                 preferred_element_type=F32)


def _matmul(a, w, *, tm, tn, n_out=None, residual=None, transposed=False, vmem_mib=48, name="matmul"):
    m, k = a.shape
    if transposed:
        assert residual is None
        n = w.shape[0] if n_out is None else n_out
        w_spec = pl.BlockSpec((tn, k), lambda i, j: (j, 0))
        kern = _mm_nt_kernel
    else:
        n = w.shape[1] if n_out is None else n_out
        w_spec = pl.BlockSpec((k, tn), lambda i, j: (0, j))
        kern = _mm_kernel
    in_specs = [pl.BlockSpec((tm, k), lambda i, j: (i, 0)), w_spec]
    args = [a, w]
    if residual is not None:
        in_specs.append(pl.BlockSpec((tm, tn), lambda i, j: (i, j)))
        args.append(residual)
        kern = _mm_res_kernel
    return pl.pallas_call(
        kern,
        grid=(m // tm, n // tn),
        in_specs=in_specs,
        out_specs=pl.BlockSpec((tm, tn), lambda i, j: (i, j)),
        out_shape=jax.ShapeDtypeStruct((m, n), F32),
        compiler_params=_params(("parallel", "arbitrary"), vmem_mib),
        name=name,
    )(*args)


def _swiglu_kernel(a_ref, wg_ref, wu_ref, o_ref):
    a = a_ref[...]
    g = jnp.dot(a, wg_ref[...].astype(BF16), preferred_element_type=F32)
    u = jnp.dot(a, wu_ref[...].astype(BF16), preferred_element_type=F32)
    o_ref[...] = (g * jax.nn.sigmoid(g) * u).astype(o_ref.dtype)


def _swiglu(a, w, *, tm, tn):
    m, k = a.shape
    f = w.shape[1] // 2
    nb = f // tn
    return pl.pallas_call(
        _swiglu_kernel,
        grid=(m // tm, nb),
        in_specs=[
            pl.BlockSpec((tm, k), lambda i, j: (i, 0)),
            pl.BlockSpec((k, tn), lambda i, j: (0, j)),
            pl.BlockSpec((k, tn), lambda i, j: (0, j + nb)),
        ],
        out_specs=pl.BlockSpec((tm, tn), lambda i, j: (i, j)),
        out_shape=jax.ShapeDtypeStruct((m, f), BF16),
        compiler_params=_params(("parallel", "arbitrary"), 56),
        name="swiglu",
    )(a, w, w)


def _mix_kernel(rw_ref, at_ref, wr_ref, wa_ref, ga_ref, gb_ref, o_ref):
    rw = jnp.dot(rw_ref[...], wr_ref[...], preferred_element_type=F32)
    at = jnp.dot(at_ref[...], wa_ref[...], preferred_element_type=F32)
    mixed = jax.nn.sigmoid(ga_ref[...]) * rw + jax.nn.sigmoid(gb_ref[...]) * at
    o_ref[...] = mixed.astype(o_ref.dtype)


def _mix(rw, at, w_rw, w_at, z, gate_col0, *, tm, tn):
    m, kr = rw.shape
    ka = at.shape[1]
    d = w_rw.shape[1]
    nb = d // tn
    g0 = gate_col0 // tn
    return pl.pallas_call(
        _mix_kernel,
        grid=(m // tm, nb),
        in_specs=[
            pl.BlockSpec((tm, kr), lambda i, j: (i, 0)),
            pl.BlockSpec((tm, ka), lambda i, j: (i, 0)),
            pl.BlockSpec((kr, tn), lambda i, j: (0, j)),
            pl.BlockSpec((ka, tn), lambda i, j: (0, j)),
            pl.BlockSpec((tm, tn), lambda i, j: (i, g0 + j)),
            pl.BlockSpec((tm, tn), lambda i, j: (i, g0 + nb + j)),
        ],
        out_specs=pl.BlockSpec((tm, tn), lambda i, j: (i, j)),
        out_shape=jax.ShapeDtypeStruct((m, d), BF16),
        compiler_params=_params(("parallel", "arbitrary"), 48),
        name="mix",
    )(rw, at, w_rw, w_at, z, z)


def _split_bf16(x):
    hi = x.astype(BF16)
    lo = (x - hi.astype(F32)).astype(BF16)
    return hi, lo


def _seg_sum_bcast(x, seg, seg_t):
    hi, lo = _split_bf16(x)
    s = jnp.dot(hi, seg, preferred_element_type=F32) + jnp.dot(lo, seg, preferred_element_type=F32)
    shi, slo = _split_bf16(s)
    return jnp.dot(shi, seg_t, preferred_element_type=F32) + jnp.dot(slo, seg_t, preferred_element_type=F32)


def _prep_body(cur, prev, mu, w0, wup, a0, aup, gup, k_k, k_a, r_k, seg, seg_t, outs):
    r_ref, w_ref, k_ref, v_ref, kkn_ref, b_ref, g_ref, bonus_ref = outs
    xs = {n: cur[n] + (prev[n] - cur[n]) * mu[n] for n in ("r", "k", "v", "t")}
    r, k, v, t = xs["r"], xs["k"], xs["v"], xs["t"]
    wl = w0 + jnp.dot(jnp.tanh(t).astype(BF16), wup, preferred_element_type=F32)
    w_log = -(jnp.maximum(-wl, 0.0) + jnp.log(1.0 + jnp.exp(-jnp.abs(wl)))) - 0.5
    decay = jnp.exp(-jnp.exp(w_log))
    a = jax.nn.sigmoid(a0 + jnp.dot(t.astype(BF16), aup, preferred_element_type=F32))
    g = jnp.dot(jax.nn.sigmoid(t).astype(BF16), gup, preferred_element_type=F32)
    kk = k * k_k
    kk = kk * lax.rsqrt(jnp.maximum(_seg_sum_bcast(kk * kk, seg, seg_t), 1e-24))
    k_mod = k * (1.0 + (a - 1.0) * k_a)
    bonus = _seg_sum_bcast(r * k_mod * r_k, seg, seg_t) * v
    r_ref[...] = r
    w_ref[...] = decay
    k_ref[...] = k_mod
    v_ref[...] = v
    kkn_ref[...] = -kk
    b_ref[...] = kk * a
    g_ref[...] = g
    bonus_ref[...] = bonus


def _prep_roll_kernel(zr, zk, zv, zt, br, bk, bv, bt, sr, sk, sv, st, mr, mk, mv, mt, w0, wup, a0, aup, gup, k_k,
                      k_a, r_k, seg, seg_t, *outs, tiles_per_seq):
    cur = {"r": zr[...], "k": zk[...], "v": zv[...], "t": zt[...]}
    seq_start = pl.program_id(0) % tiles_per_seq == 0
    prev = {}
    for n, bref, sref in (("r", br, sr), ("k", bk, sk), ("v", bv, sv), ("t", bt, st)):
        x = cur[n]
        rolled = pltpu.roll(x, 1, 0)
        row0 = lax.broadcasted_iota(jnp.int32, x.shape, 0) == 0
        first = jnp.where(seq_start, sref[...], bref[SUBLANES - 1:, :])
        prev[n] = jnp.where(row0, first, rolled)
    mu = {"r": mr[...], "k": mk[...], "v": mv[...], "t": mt[...]}
    _prep_body(cur, prev, mu, w0[...], wup[...], a0[...], aup[...], gup[...], k_k[...], k_a[...], r_k[...],
               seg[...], seg_t[...], outs)


def _prep_given_kernel(zr, zk, zv, zt, pr, pk, pv, pt, mr, mk, mv, mt, w0, wup, a0, aup, gup, k_k, k_a, r_k, seg,
                       seg_t, *outs):
    cur = {"r": zr[...], "k": zk[...], "v": zv[...], "t": zt[...]}
    prev = {"r": pr[...], "k": pk[...], "v": pv[...], "t": pt[...]}
    mu = {"r": mr[...], "k": mk[...], "v": mv[...], "t": mt[...]}
    _prep_body(cur, prev, mu, w0[...], wup[...], a0[...], aup[...], gup[...], k_k[...], k_a[...], r_k[...],
               seg[...], seg_t[...], outs)


def _rwkv_prep(z, row0, rows, rw_col0, prev_arr, shift_arr, seq_len, p, *, tm):
    c = p["w0"].shape[1]
    hc = c // 2
    nrb = rows // tm
    rb0 = row0 // tm
    cb = rw_col0 // hc
    tb = (rw_col0 + 3 * c) // 512
    zspec = lambda off: pl.BlockSpec((tm, hc), lambda i, j: (rb0 + i, cb + off + j))
    in_specs = [zspec(0), zspec(2), zspec(4), pl.BlockSpec((tm, 512), lambda i, j: (rb0 + i, tb))]
    args = [z, z, z, z]
    if prev_arr is not None:
        kern = _prep_given_kernel
        pspec = lambda off: pl.BlockSpec((tm, hc), lambda i, j: (i, off + j))
        in_specs += [pspec(0), pspec(2), pspec(4), pl.BlockSpec((tm, 512), lambda i, j: (i, 3 * c // 512))]
        args += [prev_arr] * 4
    else:
        tiles_per_seq = seq_len // tm
        kern = functools.partial(_prep_roll_kernel, tiles_per_seq=tiles_per_seq)
        before = lambda i: jnp.maximum((rb0 + i) * (tm // SUBLANES) - 1, 0)
        bspec = lambda off: pl.BlockSpec((SUBLANES, hc), lambda i, j: (before(i), cb + off + j))
        in_specs += [bspec(0), bspec(2), bspec(4), pl.BlockSpec((SUBLANES, 512), lambda i, j: (before(i), tb))]
        args += [z] * 4
        sspec = lambda off: pl.BlockSpec((None, 1, hc), lambda i, j: (i // tiles_per_seq, 0, off + j))
        in_specs += [sspec(0), sspec(2), sspec(4),
                     pl.BlockSpec((None, 1, 512), lambda i, j: (i // tiles_per_seq, 0, 3 * c // 512))]
        args += [shift_arr] * 4
    mspec = lambda off: pl.BlockSpec((1, hc), lambda i, j: (0, off + j))
    in_specs += [mspec(0), mspec(2), mspec(4), pl.BlockSpec((1, 512), lambda i, j: (0, 3 * c // 512))]
    args += [p["mu"]] * 4
    vec = pl.BlockSpec((1, hc), lambda i, j: (0, j))
    lowrank = pl.BlockSpec((512, hc), lambda i, j: (0, j))
    in_specs += [vec, lowrank, vec, lowrank, lowrank, vec, vec, vec,
                 pl.BlockSpec((hc, LANES), lambda i, j: (0, 0)), pl.BlockSpec((LANES, hc), lambda i, j: (0, 0))]
    args += [p["w0"], p["wup"], p["a0"], p["aup"], p["gup"], p["k_k"], p["k_a"], p["r_k"], p["seg"], p["seg_t"]]
    out_spec = pl.BlockSpec((tm, hc), lambda i, j: (i, j))
    return pl.pallas_call(
        kern,
        grid=(nrb, 2),
        in_specs=in_specs,
        out_specs=[out_spec] * 8,
        out_shape=[jax.ShapeDtypeStruct((rows, c), F32)] * 8,
        compiler_params=_params(("parallel", "arbitrary"), 48),
        name="rwkv_prep",
    )(*args)


N_PAIRS = 16
N_V_MXU = 0


def _trunc_bf16(x):
    return lax.bitcast_convert_type(lax.bitcast_convert_type(x, jnp.int32) & jnp.int32(-65536), F32)


def _wkv_kernel(r_ref, w_ref, k_ref, v_ref, kkn_ref, b_ref, s0_ref, y_ref, st_ref, s_scr, y_scr, *, tc):
    ci = pl.program_id(1)

    @pl.when(ci == 0)
    def _():
        for p in range(N_PAIRS):
            s_scr[p] = jnp.concatenate([s0_ref[2 * p], s0_ref[2 * p + 1]], axis=1)

    y_scr[...] = jnp.zeros(y_scr.shape, F32)
    lane = lax.broadcasted_iota(jnp.int32, (RW_HEAD, LANES), 1)
    sub = lax.broadcasted_iota(jnp.int32, (RW_HEAD, LANES), 0)
    head_a = lane < RW_HEAD
    diag_a = lane == sub
    diag_b = lane == sub + RW_HEAD
    diag = (lane & (RW_HEAD - 1)) == sub
    n_v = N_V_MXU * RW_HEAD
    wr = lax.broadcasted_iota(jnp.int32, (2 * LANES, 2 * LANES), 0)
    wc = lax.broadcasted_iota(jnp.int32, (2 * LANES, 2 * LANES), 1)
    ones_bd = ((wr // RW_HEAD) == (wc // RW_HEAD)).astype(F32)
    n_sa = N_PAIRS * RW_HEAD

    def readout(res, ymask):
        for p in range(N_PAIRS):
            blk = n_sa + RW_HEAD * (p // 2)
            yb = res[blk:blk + RW_HEAD, LANES * (p % 2):LANES * (p % 2 + 1)]
            y_scr[p] = jnp.where(ymask, yb, y_scr[p])

    def step_group(gi, carry):
        t0 = pl.multiple_of(gi * SUBLANES, SUBLANES)
        tp0 = pl.multiple_of(jnp.maximum(gi - 1, 0) * SUBLANES, SUBLANES)
        rows8 = lambda ref, p: ref[pl.ds(t0, SUBLANES), LANES * p:LANES * (p + 1)]
        kkn8 = [rows8(kkn_ref, p) for p in range(N_PAIRS)]
        r8 = [rows8(r_ref, p) for p in range(N_PAIRS)]
        v8 = [rows8(v_ref, p) for p in range(N_PAIRS)]
        w8 = [rows8(w_ref, p) for p in range(N_PAIRS)]
        b8 = [rows8(b_ref, p) for p in range(N_PAIRS)]
        k8 = [rows8(k_ref, p) for p in range(N_PAIRS)]
        r_last = [r_ref[pl.ds(tp0, SUBLANES), LANES * p:LANES * (p + 1)][SUBLANES - 1:] for p in range(N_PAIRS)]
        for u in range(SUBLANES):
            ymask = (lane & (RW_HEAD - 1)) == (t0 + (u - 1))
            lhs_sa, lhs_y, lhs_v, qs = [], [], [], []
            for p in range(N_PAIRS):
                s_p = s_scr[p]
                pr = s_p * kkn8[p][u:u + 1]
                pr_hi = _trunc_bf16(pr)
                lhs_sa.append(jnp.concatenate([pr_hi, pr - pr_hi], axis=1))
                qs.append(s_p * (r8[p][u - 1:u] if u else r_last[p]))
                if p % 2:
                    lhs_y.append(jnp.concatenate(qs[-2:], axis=1))
                if p < N_V_MXU:
                    vrow = v8[p][u:u + 1]
                    v_hi = _trunc_bf16(vrow)
                    lhs_v.append(jnp.concatenate(
                        [jnp.where(diag, v_hi, 0.0), jnp.where(diag, vrow - v_hi, 0.0)], axis=1))
            res = jnp.dot(jnp.concatenate(lhs_v + lhs_sa + lhs_y, axis=0), ones_bd, preferred_element_type=F32)
            res_v, res = res[:n_v], res[n_v:]
            for p in range(N_PAIRS):
                sa2 = res[RW_HEAD * p:RW_HEAD * (p + 1)]
                sa = sa2[:, :LANES] + sa2[:, LANES:]
                if p < N_V_MXU:
                    v2 = res_v[RW_HEAD * p:RW_HEAD * (p + 1)]
                    vcol = v2[:, :LANES] + v2[:, LANES:]
                else:
                    vrow = v8[p][u:u + 1]
                    va = jnp.sum(jnp.where(diag_a, vrow, 0.0), axis=1, keepdims=True)
                    vb = jnp.sum(jnp.where(diag_b, vrow, 0.0), axis=1, keepdims=True)
                    vcol = jnp.where(head_a, va, vb)
                s_scr[p] = s_scr[p] * w8[p][u:u + 1] + sa * b8[p][u:u + 1] + vcol * k8[p][u:u + 1]
            readout(res, ymask)
        return carry

    lax.fori_loop(0, tc // SUBLANES, step_group, 0)

    qs = [s_scr[p] * r_ref[pl.ds(tc - 1, 1), slice(LANES * p, LANES * (p + 1))] for p in range(N_PAIRS)]
    lhs_y = [jnp.concatenate(qs[p:p + 2], axis=1) for p in range(0, N_PAIRS, 2)]
    res = jnp.dot(jnp.concatenate(lhs_y, axis=0), ones_bd, preferred_element_type=F32)
    readout(jnp.concatenate([jnp.zeros((n_sa, 2 * LANES), F32), res], axis=0), (lane & (RW_HEAD - 1)) == (tc - 1))
    for p in range(N_PAIRS):
        yt = y_scr[p].T
        rows = jnp.concatenate([yt[:RW_HEAD], yt[RW_HEAD:]], axis=1)
        y_ref[:, LANES * p:LANES * (p + 1)] = rows[:tc]

    @pl.when(ci == pl.num_programs(1) - 1)
    def _():
        for p in range(N_PAIRS):
            s_p = s_scr[p]
            st_ref[2 * p] = s_p[:, :RW_HEAD]
            st_ref[2 * p + 1] = s_p[:, RW_HEAD:]


def _wkv_scan(r, w, k, v, kkn, b, s0, *, nb, t, tc):
    c = r.shape[1]
    h = c // RW_HEAD
    nc = t // tc
    row = pl.BlockSpec((tc, c), lambda bi, ci: (bi * nc + ci, 0))
    st = pl.BlockSpec((None, None, h, RW_HEAD, RW_HEAD), lambda bi, ci: (0, bi, 0, 0, 0))
    return pl.pallas_call(
        functools.partial(_wkv_kernel, tc=tc),
        grid=(nb, nc),
        in_specs=[row] * 6 + [st],
        out_specs=[row, st],
        out_shape=[jax.ShapeDtypeStruct((nb * t, c), F32), jax.ShapeDtypeStruct((1, nb, h, RW_HEAD, RW_HEAD), F32)],
        scratch_shapes=[pltpu.VMEM((N_PAIRS, RW_HEAD, LANES), F32), pltpu.VMEM((N_PAIRS, RW_HEAD, LANES), F32)],
        compiler_params=_params(("parallel", "arbitrary"), 48),
        name="wkv_scan",
    )(r, w, k, v, kkn, b, s0)


def _wkv_lanes_kernel(r_ref, w_ref, k_ref, v_ref, kkn_ref, b_ref, s0_ref, y_ref, st_ref, s_scr, *, steps):
    ci = pl.program_id(1)

    @pl.when(ci == 0)
    def _():
        s_scr[...] = s0_ref[...]

    def value_group(ig, carry):
        i0 = pl.multiple_of(ig * SUBLANES, SUBLANES)
        v8 = [v_ref[s, pl.ds(i0, SUBLANES), :] for s in range(steps)]
        ys = [[] for _ in range(steps)]
        for ii in range(SUBLANES):
            s_i = s_scr[i0 + ii]
            for s in range(steps):
                sa = jnp.sum(s_i * kkn_ref[s], axis=0, keepdims=True)
                s_i = s_i * w_ref[s] + sa * b_ref[s] + v8[s][ii:ii + 1] * k_ref[s]
                ys[s].append(jnp.sum(s_i * r_ref[s], axis=0, keepdims=True))
            s_scr[i0 + ii] = s_i
        for s in range(steps):
            y_ref[s, pl.ds(i0, SUBLANES), :] = jnp.concatenate(ys[s], axis=0)
        return carry

    lax.fori_loop(0, RW_HEAD // SUBLANES, value_group, 0)

    @pl.when(ci == pl.num_programs(1) - 1)
    def _():
        st_ref[...] = s_scr[...]


def _wkv_scan_lanes(vecs, s0, *, tc):
    t, c, nl = vecs[0].shape
    g = c // RW_HEAD
    vec = pl.BlockSpec((tc, RW_HEAD, nl), lambda gi, ci: (ci, gi, 0))
    st = pl.BlockSpec((None, RW_HEAD, RW_HEAD, nl), lambda gi, ci: (gi, 0, 0, 0))
    return pl.pallas_call(
        functools.partial(_wkv_lanes_kernel, steps=tc),
        grid=(g, t // tc),
        in_specs=[vec] * 6 + [st],
        out_specs=[vec, st],
        out_shape=[jax.ShapeDtypeStruct((t, c, nl), F32), jax.ShapeDtypeStruct((g, RW_HEAD, RW_HEAD, nl), F32)],
        scratch_shapes=[pltpu.VMEM((RW_HEAD, RW_HEAD, nl), F32)],
        compiler_params=_params(("parallel", "arbitrary"), 48),
        name="wkv_scan_lanes",
    )(*vecs, s0)


def _rw_post_kernel(y_ref, g_ref, bonus_ref, lw_ref, lb_ref, seg_ref, segt_ref, o_ref):
    y = y_ref[...]
    seg = seg_ref[...]
    seg_t = segt_ref[...]
    inv = 1.0 / RW_HEAD
    mu = _seg_sum_bcast(y, seg, seg_t) * inv
    d = y - mu
    var = _seg_sum_bcast(d * d, seg, seg_t) * inv
    yn = d * lax.rsqrt(var + GN_EPS)
    o_ref[...] = ((yn * lw_ref[...] + lb_ref[...] + bonus_ref[...]) * g_ref[...]).astype(o_ref.dtype)


def _rw_post(y, g, bonus, lnx_w, lnx_b, seg, seg_t, *, tm):
    m, c = y.shape
    blk = pl.BlockSpec((tm, c), lambda i: (i, 0))
    vec = pl.BlockSpec((1, c), lambda i: (0, 0))
    return pl.pallas_call(
        _rw_post_kernel,
        grid=(m // tm,),
        in_specs=[blk, blk, blk, vec, vec, pl.BlockSpec((c, LANES), lambda i: (0, 0)),
                  pl.BlockSpec((LANES, c), lambda i: (0, 0))],
        out_specs=blk,
        out_shape=jax.ShapeDtypeStruct((m, c), BF16),
        compiler_params=_params(("parallel",), 48),
        name="rw_post",
    )(y, g, bonus, lnx_w, lnx_b, seg, seg_t)


QB = 128


def _attn_prompt_kernel(sl_ref, q0, k0, v0, q1, k1, v1, q2, k2, v2, out_ref, o_scr, l_scr, *, t):
    h = pl.program_id(1)
    scale = ATT_HEAD_DIM ** -0.5
    iq = lax.broadcasted_iota(jnp.int32, (QB, QB), 0)
    ik = lax.broadcasted_iota(jnp.int32, (QB, QB), 1)
    d_cur = (iq - ik).astype(F32)
    refs = ((q0, k0, v0), (q1, k1, v1), (q2, k2, v2))
    for g, (win, dil) in enumerate(ATT_GROUPS):
        q_ref, k_ref, v_ref = refs[g]
        slope = sl_ref[g * ATT_HEADS + h]
        nblk = t // dil // QB
        bias_cur = jnp.where(ik <= iq, -slope * dil * d_cur, NEG_INF)
        bias_prev = jnp.where(ik >= iq, -slope * dil * (d_cur + QB), NEG_INF)

        def block(idx, carry, q_ref=q_ref, k_ref=k_ref, v_ref=v_ref, dil=dil, nblk=nblk, g=g,
                  bias_cur=bias_cur, bias_prev=bias_prev):
            res = idx // nblk
            qb = idx % nblk
            base = res + qb * (QB * dil)
            pbase = res + jnp.maximum(qb - 1, 0) * (QB * dil)
            rows = pl.ds(base, QB, stride=dil) if dil > 1 else pl.ds(base, QB)
            prows = pl.ds(pbase, QB, stride=dil) if dil > 1 else pl.ds(pbase, QB)
            q = q_ref[rows, :].astype(BF16)
            kc = k_ref[rows, :].astype(BF16)
            dims = (((1,), (1,)), ((), ()))
            s_c = lax.dot_general(q, kc, dims, preferred_element_type=F32) * scale + bias_cur
            m = jnp.max(s_c, axis=-1, keepdims=True)
            if nblk > 1:
                kp = k_ref[prows, :].astype(BF16)
                s_p = lax.dot_general(q, kp, dims, preferred_element_type=F32) * scale + jnp.where(
                    qb > 0, bias_prev, NEG_INF)
                m = jnp.maximum(m, jnp.max(s_p, axis=-1, keepdims=True))
            p_c = jnp.exp(s_c - m)
            l = jnp.sum(p_c, axis=-1, keepdims=True)
            o = jnp.dot(p_c.astype(BF16), v_ref[rows, :].astype(BF16), preferred_element_type=F32)
            if nblk > 1:
                p_p = jnp.exp(s_p - m)
                l = l + jnp.sum(p_p, axis=-1, keepdims=True)
                o = o + jnp.dot(p_p.astype(BF16), v_ref[prows, :].astype(BF16), preferred_element_type=F32)
            o_scr[g, rows, :] = o / l
            l_scr[g, rows, :] = jnp.broadcast_to(m + jnp.log(l), (QB, ATT_HEAD_DIM))
            return carry

        lax.fori_loop(0, dil * nblk, block, 0, unroll=4)

    l0, l1, l2 = l_scr[0], l_scr[1], l_scr[2]
    m = jnp.maximum(jnp.maximum(l0, l1), l2)
    e0, e1, e2 = jnp.exp(l0 - m), jnp.exp(l1 - m), jnp.exp(l2 - m)
    den = e0 + e1 + e2
    out_ref[...] = ((e0 * o_scr[0] + e1 * o_scr[1] + e2 * o_scr[2]) / den).astype(out_ref.dtype)


def _attn_prompt(z, slopes, *, nb, t):
    specs = []
    for g in range(len(ATT_GROUPS)):
        for c in range(3):
            cb = (g * 3 + c) * ATT_HEADS
            specs.append(pl.BlockSpec((t, ATT_HEAD_DIM), lambda b, h, sl, cb=cb: (b, cb + h)))
    grid_spec = pltpu.PrefetchScalarGridSpec(
        num_scalar_prefetch=1,
        grid=(nb, ATT_HEADS),
        in_specs=specs,
        out_specs=pl.BlockSpec((t, ATT_HEAD_DIM), lambda b, h, sl: (b, h)),
        scratch_shapes=[pltpu.VMEM((3, t, ATT_HEAD_DIM), F32), pltpu.VMEM((3, t, ATT_HEAD_DIM), F32)],
    )
    return pl.pallas_call(
        functools.partial(_attn_prompt_kernel, t=t),
        grid_spec=grid_spec,
        out_shape=jax.ShapeDtypeStruct((nb * t, ATT_HEADS * ATT_HEAD_DIM), BF16),
        compiler_params=_params(("parallel", "arbitrary"), 48),
        name="attn_prompt",
    )(slopes, *([z] * 9))


N_NEW = 8
N_BAND = 128


def _attn_sample_kernel(q0, q1, q2, n0, n1, n2, c0, c1, c2, sl_ref, out_ref, sc_scr, o_scr, l_scr):
    scale = ATT_HEAD_DIM ** -0.5
    qs, news, caches = (q0, q1, q2), (n0, n1, n2), (c0, c1, c2)
    tile = (ATT_HEADS, ATT_HEAD_DIM)
    for g, (win, dil) in enumerate(ATT_GROUPS):
        q_ref, n_ref, c_ref = qs[g], news[g], caches[g]
        slope = sl_ref[g]
        past = N_BAND * dil
        qv = [q_ref[s] * scale for s in range(N_NEW)]
        if dil == 1:
            combos = [(0, s) for s in range(N_NEW)]
        elif dil == 4:
            combos = [(r, s) for r in range(4) for s in (r, r + 4)]
        else:
            combos = [(r, r) for r in range(N_NEW)]

        def row_tiles(m, r):
            if dil == 1:
                return c_ref[m, 0], c_ref[m, 1]
            if dil == 4:
                a = m * 4 + r
                return c_ref[a, 0], c_ref[a, 1]
            return c_ref[m, r, 0], c_ref[m, r, 1]

        def score(m, r, s):
            a = m * dil + r
            k_t, _ = row_tiles(m, r)
            sc = jnp.sum(qv[s] * k_t, axis=-1, keepdims=True) - slope * jnp.asarray(past + s - a, F32)
            if dil == 1:
                sc = jnp.where(m >= s, sc, NEG_INF)
            elif dil == 4 and s >= 4:
                sc = jnp.where(m >= 1, sc, NEG_INF)
            return jnp.broadcast_to(sc, tile)

        new_sc = []
        for s in range(N_NEW):
            lst = []
            for j in range(s // dil + 1):
                s2 = s - j * dil
                sc = jnp.sum(qv[s] * n_ref[s2, 0], axis=-1, keepdims=True) - slope * float(j * dil)
                lst.append((s2, jnp.broadcast_to(sc, tile)))
            new_sc.append(lst)
        m0 = [functools.reduce(jnp.maximum, [x for _, x in new_sc[s]]) for s in range(N_NEW)]

        def pass1(m, mx):
            mx = list(mx)
            for ci, (r, s) in enumerate(combos):
                sc = score(m, r, s)
                sc_scr[m, ci] = sc
                mx[s] = jnp.maximum(mx[s], sc)
            return tuple(mx)

        mx = lax.fori_loop(0, N_BAND, pass1, tuple(m0), unroll=32)

        l_init, o_init = [], []
        for s in range(N_NEW):
            l_s = jnp.zeros(tile, F32)
            o_s = jnp.zeros(tile, F32)
            for s2, sc in new_sc[s]:
                p = jnp.exp(sc - mx[s])
                l_s = l_s + p
                o_s = o_s + p * n_ref[s2, 1]
            l_init.append(l_s)
            o_init.append(o_s)

        def pass2(m, carry):
            ls, os_ = list(carry[0]), list(carry[1])
            for ci, (r, s) in enumerate(combos):
                _, v_t = row_tiles(m, r)
                p = jnp.exp(sc_scr[m, ci] - mx[s])
                ls[s] = ls[s] + p
                os_[s] = os_[s] + p * v_t
            return tuple(ls), tuple(os_)

        ls, os_ = lax.fori_loop(0, N_BAND, pass2, (tuple(l_init), tuple(o_init)), unroll=8)
        for s in range(N_NEW):
            o_scr[g, s] = os_[s] / ls[s]
            l_scr[g, s] = mx[s] + jnp.log(ls[s])

    for s in range(N_NEW):
        l0, l1, l2 = l_scr[0, s], l_scr[1, s], l_scr[2, s]
        m = jnp.maximum(jnp.maximum(l0, l1), l2)
        e0, e1, e2 = jnp.exp(l0 - m), jnp.exp(l1 - m), jnp.exp(l2 - m)
        out_ref[s] = (e0 * o_scr[0, s] + e1 * o_scr[1, s] + e2 * o_scr[2, s]) / (e0 + e1 + e2)


def _attn_sample(qs, news, caches, slope_tiles):
    nb = qs[0].shape[0]
    he = (ATT_HEADS, ATT_HEAD_DIM)
    qspec = pl.BlockSpec((None, N_NEW) + he, lambda b: (b, 0, 0, 0))
    nspec = pl.BlockSpec((None, N_NEW, 2) + he, lambda b: (b, 0, 0, 0, 0))
    c2 = caches[2]
    dil2 = ATT_GROUPS[2][1]
    c2v = c2.reshape(nb, c2.shape[1] // dil2, dil2, 2, *he)
    cspecs = [
        pl.BlockSpec((None, caches[0].shape[1], 2) + he, lambda b: (b, 0, 0, 0, 0)),
        pl.BlockSpec((None, caches[1].shape[1], 2) + he, lambda b: (b, 0, 0, 0, 0)),
        pl.BlockSpec((None, N_BAND, N_NEW, 2) + he, lambda b: (b, 0, 0, 0, 0, 0)),
    ]
    return pl.pallas_call(
        _attn_sample_kernel,
        grid=(nb,),
        in_specs=[qspec] * 3 + [nspec] * 3 + cspecs + [pl.BlockSpec((3,) + he, lambda b: (0, 0, 0))],
        out_specs=pl.BlockSpec((None, N_NEW) + he, lambda b: (b, 0, 0, 0)),
        out_shape=jax.ShapeDtypeStruct((nb, N_NEW) + he, F32),
        scratch_shapes=[pltpu.VMEM((N_BAND, N_NEW) + he, F32), pltpu.VMEM((3, N_NEW) + he, F32),
                        pltpu.VMEM((3, N_NEW) + he, F32)],
        compiler_params=_params(("parallel",), 48),
        name="attn_sample",
    )(*qs, *news, caches[0], caches[1], c2v, slope_tiles)


def kernel(x_prompt, x_sample, cache_kv_0, cache_kv_1, cache_kv_2, state_wkv, state_shift, norm_mix, w_in, rw_mu,
           rw_w0, rw_w_up, rw_a0, rw_a_up, rw_g_up, rw_k_k, rw_k_a, rw_r_k, rw_lnx_w, rw_lnx_b, rw_out, att_out,
           w_o, norm_ffn, w_ffn_in, w_ffn_out, norm_final):
    depth = w_in.shape[0]
    assert depth == 1, "single-layer stack"
    bp, tp, d = x_prompt.shape
    bs, ts, _ = x_sample.shape
    mp, ms = bp * tp, bs * ts
    assert ts == N_NEW
    c = rw_w0.shape[1]
    r_dec, r_aaa, r_gate = rw_w_up.shape[1], rw_a_up.shape[1], rw_g_up.shape[1]
    rw_cols = 3 * c + r_dec + r_aaa + r_gate
    att_cols = len(ATT_GROUPS) * 3 * ATT_HEADS * ATT_HEAD_DIM
    att_w = ATT_HEADS * ATT_HEAD_DIM
    pad = (-(att_cols + rw_cols)) % LANES
    rwp = rw_cols + pad
    assert rwp - 3 * c == 512
    nz = att_cols + rwp

    tail = lambda w, off: jnp.zeros((512, c), F32).at[off:off + w.shape[0]].set(w).astype(BF16)
    row = lambda v, n: jnp.pad(v.reshape(1, -1), ((0, 0), (0, n - v.size)))
    seg = (jnp.arange(c // 2)[:, None] // RW_HEAD == jnp.arange(LANES)[None, :]).astype(BF16)
    prm = {
        "mu": row(rw_mu[0], rwp), "w0": row(rw_w0[0], c), "a0": row(rw_a0[0], c),
        "wup": tail(rw_w_up[0], 0), "aup": tail(rw_a_up[0], r_dec), "gup": tail(rw_g_up[0], r_dec + r_aaa),
        "k_k": row(rw_k_k[0], c), "k_a": row(rw_k_a[0], c), "r_k": row(rw_r_k[0].reshape(-1), c),
        "seg": seg, "seg_t": seg.T,
    }
    seg_full = (jnp.arange(c)[:, None] // RW_HEAD == jnp.arange(LANES)[None, :]).astype(BF16)

    x = jnp.concatenate([x_prompt.reshape(mp, d), x_sample.reshape(ms, d)], axis=0)
    m_all = mp + ms
    hmix = _rmsnorm(x, norm_mix[0], BF16)
    wi_t = jnp.swapaxes(w_in, 1, 2)[0]
    z = _matmul(hmix, wi_t, tm=1024, tn=512, n_out=nz, transposed=True, vmem_mib=56, name="w_in")
    zg = _matmul(hmix, wi_t[att_cols + rw_cols:], tm=1024, tn=512, transposed=True, vmem_mib=56,
                 name="w_gate")
    z_s = z[mp:].reshape(bs, ts, nz)

    new_kv_p, new_kv_s = [], []
    for g, (win, dil) in enumerate(ATT_GROUPS):
        lo = g * 3 * att_w + att_w
        keep = min(win, tp)
        kv = jnp.stack([z[(b + 1) * tp - keep:(b + 1) * tp, lo:lo + 2 * att_w] for b in range(bp)])
        new_kv_p.append(kv.reshape(bp, keep, 2, ATT_HEADS, ATT_HEAD_DIM)[None])
        new_kv_s.append(z_s[:, :, lo:lo + 2 * att_w].reshape(bs, ts, 2, ATT_HEADS, ATT_HEAD_DIM)[None])
    shift_p = jnp.stack([z[(b + 1) * tp - 1, att_cols:att_cols + rw_cols] for b in range(bp)])[None]
    shift_s = z_s[:, -1, att_cols:att_cols + rw_cols][None]

    tm_prep = 256
    prep_p = _rwkv_prep(z, 0, mp, att_cols, None, jnp.zeros((bp, 1, rwp), F32), tp, prm, tm=tm_prep)
    cols_s = z_s[:, :, att_cols:att_cols + rwp]
    shift_in = jnp.pad(state_shift[0], ((0, 0), (0, pad)))
    prev_s = jnp.concatenate([shift_in[:, None, :], cols_s[:, :-1]], axis=1).reshape(ms, rwp)
    prep_s = _rwkv_prep(z, mp, ms, att_cols, prev_s, None, ts, prm, tm=tm_prep)

    h_rw = c // RW_HEAD
    assert bp * h_rw == LANES
    to_lanes_p = lambda a: a.reshape(bp, tp, h_rw, RW_HEAD).transpose(1, 3, 0, 2).reshape(tp, RW_HEAD, LANES)
    y_t, wkv_t = _wkv_scan_lanes([to_lanes_p(a) for a in prep_p[:6]], jnp.zeros((1, RW_HEAD, RW_HEAD, LANES), F32),
                                 tc=16)
    y_p = y_t.reshape(tp, RW_HEAD, bp, h_rw).transpose(2, 0, 3, 1).reshape(mp, c)
    wkv_p = wkv_t.reshape(RW_HEAD, RW_HEAD, bp, h_rw).transpose(2, 3, 0, 1)[None]
    to_lanes_s = lambda a: a.reshape(bs, ts, c).transpose(1, 2, 0)
    y_t, wkv_t = _wkv_scan_lanes([to_lanes_s(a) for a in prep_s[:6]], jnp.transpose(state_wkv[0], (1, 2, 3, 0)),
                                 tc=ts)
    y_s = y_t.transpose(2, 0, 1).reshape(ms, c)
    wkv_s = jnp.transpose(wkv_t, (3, 0, 1, 2))[None]
    lw, lb = rw_lnx_w[0].reshape(1, c), rw_lnx_b[0].reshape(1, c)
    rw_p = _rw_post(y_p, prep_p[6], prep_p[7], lw, lb, seg_full, seg_full.T, tm=256)
    rw_s = _rw_post(y_s, prep_s[6], prep_s[7], lw, lb, seg_full, seg_full.T, tm=256)
    rw_act = jnp.concatenate([rw_p, rw_s], axis=0)

    n_heads_all = len(ATT_GROUPS) * ATT_HEADS
    slopes = 2.0 ** (-8.0 * jnp.arange(1, n_heads_all + 1, dtype=F32) / n_heads_all)
    att_p = _attn_prompt(z, slopes, nb=bp, t=tp)
    qs = [z_s[:, :, g * 3 * att_w:g * 3 * att_w + att_w].reshape(bs, ts, ATT_HEADS, ATT_HEAD_DIM)
          for g in range(len(ATT_GROUPS))]
    slope_tiles = jnp.broadcast_to(slopes.reshape(len(ATT_GROUPS), ATT_HEADS, 1),
                                   (len(ATT_GROUPS), ATT_HEADS, ATT_HEAD_DIM))
    att_s = _attn_sample(qs, [kv[0] for kv in new_kv_s], [cache_kv_0[0], cache_kv_1[0], cache_kv_2[0]], slope_tiles)
    att_act = jnp.concatenate([att_p, att_s.reshape(ms, att_w).astype(BF16)], axis=0)

    mixed = _mix(rw_act, att_act, rw_out[0].astype(BF16), att_out[0].astype(BF16), zg, 0, tm=1024, tn=512)
    x1 = _matmul(mixed, w_o[0].astype(BF16), tm=1024, tn=512, residual=x, name="w_o")
    hf = _rmsnorm(x1, norm_ffn[0], BF16)
    act = _swiglu(hf, w_ffn_in[0], tm=1024, tn=256)
    x2 = _matmul(act, w_ffn_out[0].astype(BF16), tm=512, tn=256, residual=x1, vmem_mib=56, name="w_ffn_out")
    y_p = _rmsnorm(x2, norm_final, F32, 0, mp)
    y_s = _rmsnorm(x2, norm_final, F32, mp, ms)

    return (y_p.reshape(bp, tp, d), y_s.reshape(bs, ts, d),
            new_kv_p[0], new_kv_p[1], new_kv_p[2], wkv_p, shift_p,
            new_kv_s[0], new_kv_s[1], new_kv_s[2], wkv_s, shift_s)
```

```python
import functools

import jax
import jax.numpy as jnp
from jax import lax
from jax.experimental import pallas as pl
from jax.experimental.pallas import tpu as pltpu

F32 = jnp.float32
BF16 = jnp.bfloat16

RMS_EPS = 1e-6
GN_EPS = 64e-5
RW_HEAD = 64
ATT_GROUPS = ((128, 1), (512, 4), (2048, 16))
ATT_HEAD_DIM = 128
ATT_HEADS = 8
LANES = 128
SUBLANES = 8
NEG_INF = float("-inf")
LOG2_E = 1.4426950408889634


def _params(sem, vmem_mib):
    return pltpu.CompilerParams(dimension_semantics=sem, vmem_limit_bytes=vmem_mib << 20)


def _rmsnorm_kernel(x_ref, g_ref, o_ref):
    x = x_ref[...]
    ms = jnp.mean(x * x, axis=-1, keepdims=True)
    o_ref[...] = (x * lax.rsqrt(ms + RMS_EPS) * g_ref[...]).astype(o_ref.dtype)


def _rmsnorm(x, g, out_dtype, row0=0, rows=None, tm=256):
    d = x.shape[1]
    rows = x.shape[0] if rows is None else rows
    rb0 = row0 // tm
    return pl.pallas_call(
        _rmsnorm_kernel,
        grid=(rows // tm,),
        in_specs=[pl.BlockSpec((tm, d), lambda i: (rb0 + i, 0)), pl.BlockSpec((1, d), lambda i: (0, 0))],
        out_specs=pl.BlockSpec((tm, d), lambda i: (i, 0)),
        out_shape=jax.ShapeDtypeStruct((rows, d), out_dtype),
        compiler_params=_params(("parallel",), 32),
        name="rmsnorm",
    )(x, g.reshape(1, d))


def _mm_kernel(a_ref, w_ref, o_ref):
    o_ref[...] = jnp.dot(a_ref[...], w_ref[...].astype(BF16), preferred_element_type=F32).astype(o_ref.dtype)


def _mm_res_kernel(a_ref, w_ref, r_ref, o_ref):
    o_ref[...] = r_ref[...] + jnp.dot(a_ref[...], w_ref[...].astype(BF16), preferred_element_type=F32)


def _mm_nt_kernel(a_ref, wt_ref, o_ref):
    o_ref[...] = lax.dot_general(a_ref[...], wt_ref[...].astype(BF16), (((1,), (1,)), ((), ())),
                                 preferred_element_type=F32)


def _matmul(a, w, *, tm, tn, n_out=None, residual=None, transposed=False, vmem_mib=48, name="matmul"):
    m, k = a.shape
    if transposed:
        assert residual is None
        n = w.shape[0] if n_out is None else n_out
        w_spec = pl.BlockSpec((tn, k), lambda i, j: (j, 0))
        kern = _mm_nt_kernel
    else:
        n = w.shape[1] if n_out is None else n_out
        w_spec = pl.BlockSpec((k, tn), lambda i, j: (0, j))
        kern = _mm_kernel
    in_specs = [pl.BlockSpec((tm, k), lambda i, j: (i, 0)), w_spec]
    args = [a, w]
    if residual is not None:
        in_specs.append(pl.BlockSpec((tm, tn), lambda i, j: (i, j)))
        args.append(residual)
        kern = _mm_res_kernel
    return pl.pallas_call(
        kern,
        grid=(m // tm, n // tn),
        in_specs=in_specs,
        out_specs=pl.BlockSpec((tm, tn), lambda i, j: (i, j)),
        out_shape=jax.ShapeDtypeStruct((m, n), F32),
        compiler_params=_params(("parallel", "arbitrary"), vmem_mib),
        name=name,
    )(*args)


def _swiglu_kernel(a_ref, wg_ref, wu_ref, o_ref):
    a = a_ref[...]
    g = jnp.dot(a, wg_ref[...].astype(BF16), preferred_element_type=F32)
    u = jnp.dot(a, wu_ref[...].astype(BF16), preferred_element_type=F32)
    o_ref[...] = (g * jax.nn.sigmoid(g) * u).astype(o_ref.dtype)


def _swiglu(a, w, *, tm, tn):
    m, k = a.shape
    f = w.shape[1] // 2
    nb = f // tn
    return pl.pallas_call(
        _swiglu_kernel,
        grid=(m // tm, nb),
        in_specs=[
            pl.BlockSpec((tm, k), lambda i, j: (i, 0)),
            pl.BlockSpec((k, tn), lambda i, j: (0, j)),
            pl.BlockSpec((k, tn), lambda i, j: (0, j + nb)),
        ],
        out_specs=pl.BlockSpec((tm, tn), lambda i, j: (i, j)),
        out_shape=jax.ShapeDtypeStruct((m, f), BF16),
        compiler_params=_params(("parallel", "arbitrary"), 56),
        name="swiglu",
    )(a, w, w)


def _mix_kernel(rw_ref, at_ref, wr_ref, wa_ref, ga_ref, gb_ref, o_ref):
    rw = jnp.dot(rw_ref[...], wr_ref[...], preferred_element_type=F32)
    at = jnp.dot(at_ref[...], wa_ref[...], preferred_element_type=F32)
    mixed = jax.nn.sigmoid(ga_ref[...]) * rw + jax.nn.sigmoid(gb_ref[...]) * at
    o_ref[...] = mixed.astype(o_ref.dtype)


def _mix(rw, at, w_rw, w_at, z, gate_col0, *, tm, tn):
    m, kr = rw.shape
    ka = at.shape[1]
    d = w_rw.shape[1]
    nb = d // tn
    g0 = gate_col0 // tn
    return pl.pallas_call(
        _mix_kernel,
        grid=(m // tm, nb),
        in_specs=[
            pl.BlockSpec((tm, kr), lambda i, j: (i, 0)),
            pl.BlockSpec((tm, ka), lambda i, j: (i, 0)),
            pl.BlockSpec((kr, tn), lambda i, j: (0, j)),
            pl.BlockSpec((ka, tn), lambda i, j: (0, j)),
            pl.BlockSpec((tm, tn), lambda i, j: (i, g0 + j)),
            pl.BlockSpec((tm, tn), lambda i, j: (i, g0 + nb + j)),
        ],
        out_specs=pl.BlockSpec((tm, tn), lambda i, j: (i, j)),
        out_shape=jax.ShapeDtypeStruct((m, d), BF16),
        compiler_params=_params(("parallel", "arbitrary"), 48),
        name="mix",
    )(rw, at, w_rw, w_at, z, z)


def _split_bf16(x):
    hi = x.astype(BF16)
    lo = (x - hi.astype(F32)).astype(BF16)
    return hi, lo


def _seg_sum_bcast(x, seg, seg_t):
    hi, lo = _split_bf16(x)
    s = jnp.dot(hi, seg, preferred_element_type=F32) + jnp.dot(lo, seg, preferred_element_type=F32)
    shi, slo = _split_bf16(s)
    return jnp.dot(shi, seg_t, preferred_element_type=F32) + jnp.dot(slo, seg_t, preferred_element_type=F32)


def _prep_body(cur, prev, mu, w0, wup, a0, aup, gup, k_k, k_a, r_k, seg, seg_t, outs):
    r_ref, w_ref, k_ref, v_ref, kkn_ref, b_ref, g_ref, bonus_ref = outs
    xs = {n: cur[n] + (prev[n] - cur[n]) * mu[n] for n in ("r", "k", "v", "t")}
    r, k, v, t = xs["r"], xs["k"], xs["v"], xs["t"]
    wl = w0 + jnp.dot(jnp.tanh(t).astype(BF16), wup, preferred_element_type=F32)
    w_log = -(jnp.maximum(-wl, 0.0) + jnp.log(1.0 + jnp.exp(-jnp.abs(wl)))) - 0.5
    decay = jnp.exp(-jnp.exp(w_log))
    a = jax.nn.sigmoid(a0 + jnp.dot(t.astype(BF16), aup, preferred_element_type=F32))
    g = jnp.dot(jax.nn.sigmoid(t).astype(BF16), gup, preferred_element_type=F32)
    kk = k * k_k
    kk = kk * lax.rsqrt(jnp.maximum(_seg_sum_bcast(kk * kk, seg, seg_t), 1e-24))
    k_mod = k * (1.0 + (a - 1.0) * k_a)
    bonus = _seg_sum_bcast(r * k_mod * r_k, seg, seg_t) * v
    r_ref[...] = r
    w_ref[...] = decay
    k_ref[...] = k_mod
    v_ref[...] = v
    kkn_ref[...] = -kk
    b_ref[...] = kk * a
    g_ref[...] = g
    bonus_ref[...] = bonus


def _prep_roll_kernel(zr, zk, zv, zt, br, bk, bv, bt, sr, sk, sv, st, mr, mk, mv, mt, w0, wup, a0, aup, gup, k_k,
                      k_a, r_k, seg, seg_t, *outs, tiles_per_seq):
    cur = {"r": zr[...], "k": zk[...], "v": zv[...], "t": zt[...]}
    seq_start = pl.program_id(0) % tiles_per_seq == 0
    prev = {}
    for n, bref, sref in (("r", br, sr), ("k", bk, sk), ("v", bv, sv), ("t", bt, st)):
        x = cur[n]
        rolled = pltpu.roll(x, 1, 0)
        row0 = lax.broadcasted_iota(jnp.int32, x.shape, 0) == 0
        first = jnp.where(seq_start, sref[...], bref[SUBLANES - 1:, :])
        prev[n] = jnp.where(row0, first, rolled)
    mu = {"r": mr[...], "k": mk[...], "v": mv[...], "t": mt[...]}
    _prep_body(cur, prev, mu, w0[...], wup[...], a0[...], aup[...], gup[...], k_k[...], k_a[...], r_k[...],
               seg[...], seg_t[...], outs)


def _prep_given_kernel(zr, zk, zv, zt, pr, pk, pv, pt, mr, mk, mv, mt, w0, wup, a0, aup, gup, k_k, k_a, r_k, seg,
                       seg_t, *outs):
    cur = {"r": zr[...], "k": zk[...], "v": zv[...], "t": zt[...]}
    prev = {"r": pr[...], "k": pk[...], "v": pv[...], "t": pt[...]}
    mu = {"r": mr[...], "k": mk[...], "v": mv[...], "t": mt[...]}
    _prep_body(cur, prev, mu, w0[...], wup[...], a0[...], aup[...], gup[...], k_k[...], k_a[...], r_k[...],
               seg[...], seg_t[...], outs)


def _rwkv_prep(z, row0, rows, rw_col0, prev_arr, shift_arr, seq_len, p, *, tm):
    c = p["w0"].shape[1]
    hc = c // 2
    nrb = rows // tm
    rb0 = row0 // tm
    cb = rw_col0 // hc
    tb = (rw_col0 + 3 * c) // 512
    zspec = lambda off: pl.BlockSpec((tm, hc), lambda i, j: (rb0 + i, cb + off + j))
    in_specs = [zspec(0), zspec(2), zspec(4), pl.BlockSpec((tm, 512), lambda i, j: (rb0 + i, tb))]
    args = [z, z, z, z]
    if prev_arr is not None:
        kern = _prep_given_kernel
        pspec = lambda off: pl.BlockSpec((tm, hc), lambda i, j: (i, off + j))
        in_specs += [pspec(0), pspec(2), pspec(4), pl.BlockSpec((tm, 512), lambda i, j: (i, 3 * c // 512))]
        args += [prev_arr] * 4
    else:
        tiles_per_seq = seq_len // tm
        kern = functools.partial(_prep_roll_kernel, tiles_per_seq=tiles_per_seq)
        before = lambda i: jnp.maximum((rb0 + i) * (tm // SUBLANES) - 1, 0)
        bspec = lambda off: pl.BlockSpec((SUBLANES, hc), lambda i, j: (before(i), cb + off + j))
        in_specs += [bspec(0), bspec(2), bspec(4), pl.BlockSpec((SUBLANES, 512), lambda i, j: (before(i), tb))]
        args += [z] * 4
        sspec = lambda off: pl.BlockSpec((None, 1, hc), lambda i, j: (i // tiles_per_seq, 0, off + j))
        in_specs += [sspec(0), sspec(2), sspec(4),
                     pl.BlockSpec((None, 1, 512), lambda i, j: (i // tiles_per_seq, 0, 3 * c // 512))]
        args += [shift_arr] * 4
    mspec = lambda off: pl.BlockSpec((1, hc), lambda i, j: (0, off + j))
    in_specs += [mspec(0), mspec(2), mspec(4), pl.BlockSpec((1, 512), lambda i, j: (0, 3 * c // 512))]
    args += [p["mu"]] * 4
    vec = pl.BlockSpec((1, hc), lambda i, j: (0, j))
    lowrank = pl.BlockSpec((512, hc), lambda i, j: (0, j))
    in_specs += [vec, lowrank, vec, lowrank, lowrank, vec, vec, vec,
                 pl.BlockSpec((hc, LANES), lambda i, j: (0, 0)), pl.BlockSpec((LANES, hc), lambda i, j: (0, 0))]
    args += [p["w0"], p["wup"], p["a0"], p["aup"], p["gup"], p["k_k"], p["k_a"], p["r_k"], p["seg"], p["seg_t"]]
    out_spec = pl.BlockSpec((tm, hc), lambda i, j: (i, j))
    return pl.pallas_call(
        kern,
        grid=(nrb, 2),
        in_specs=in_specs,
        out_specs=[out_spec] * 8,
        out_shape=[jax.ShapeDtypeStruct((rows, c), F32)] * 8,
        compiler_params=_params(("parallel", "arbitrary"), 48),
        name="rwkv_prep",
    )(*args)


N_PAIRS = 16
N_V_MXU = 0


def _trunc_bf16(x):
    return lax.bitcast_convert_type(lax.bitcast_convert_type(x, jnp.int32) & jnp.int32(-65536), F32)


def _wkv_kernel(r_ref, w_ref, k_ref, v_ref, kkn_ref, b_ref, s0_ref, y_ref, st_ref, s_scr, y_scr, *, tc):
    ci = pl.program_id(1)

    @pl.when(ci == 0)
    def _():
        for p in range(N_PAIRS):
            s_scr[p] = jnp.concatenate([s0_ref[2 * p], s0_ref[2 * p + 1]], axis=1)

    y_scr[...] = jnp.zeros(y_scr.shape, F32)
    lane = lax.broadcasted_iota(jnp.int32, (RW_HEAD, LANES), 1)
    sub = lax.broadcasted_iota(jnp.int32, (RW_HEAD, LANES), 0)
    head_a = lane < RW_HEAD
    diag_a = lane == sub
    diag_b = lane == sub + RW_HEAD
    diag = (lane & (RW_HEAD - 1)) == sub
    n_v = N_V_MXU * RW_HEAD
    wr = lax.broadcasted_iota(jnp.int32, (2 * LANES, 2 * LANES), 0)
    wc = lax.broadcasted_iota(jnp.int32, (2 * LANES, 2 * LANES), 1)
    ones_bd = ((wr // RW_HEAD) == (wc // RW_HEAD)).astype(F32)
    n_sa = N_PAIRS * RW_HEAD

    def readout(res, ymask):
        for p in range(N_PAIRS):
            blk = n_sa + RW_HEAD * (p // 2)
            yb = res[blk:blk + RW_HEAD, LANES * (p % 2):LANES * (p % 2 + 1)]
            y_scr[p] = jnp.where(ymask, yb, y_scr[p])

    def step_group(gi, carry):
        t0 = pl.multiple_of(gi * SUBLANES, SUBLANES)
        tp0 = pl.multiple_of(jnp.maximum(gi - 1, 0) * SUBLANES, SUBLANES)
        rows8 = lambda ref, p: ref[pl.ds(t0, SUBLANES), LANES * p:LANES * (p + 1)]
        kkn8 = [rows8(kkn_ref, p) for p in range(N_PAIRS)]
        r8 = [rows8(r_ref, p) for p in range(N_PAIRS)]
        v8 = [rows8(v_ref, p) for p in range(N_PAIRS)]
        w8 = [rows8(w_ref, p) for p in range(N_PAIRS)]
        b8 = [rows8(b_ref, p) for p in range(N_PAIRS)]
        k8 = [rows8(k_ref, p) for p in range(N_PAIRS)]
        r_last = [r_ref[pl.ds(tp0, SUBLANES), LANES * p:LANES * (p + 1)][SUBLANES - 1:] for p in range(N_PAIRS)]
        for u in range(SUBLANES):
            ymask = (lane & (RW_HEAD - 1)) == (t0 + (u - 1))
            lhs_sa, lhs_y, lhs_v, qs = [], [], [], []
            for p in range(N_PAIRS):
                s_p = s_scr[p]
                pr = s_p * kkn8[p][u:u + 1]
                pr_hi = _trunc_bf16(pr)
                lhs_sa.append(jnp.concatenate([pr_hi, pr - pr_hi], axis=1))
                qs.append(s_p * (r8[p][u - 1:u] if u else r_last[p]))
                if p % 2:
                    lhs_y.append(jnp.concatenate(qs[-2:], axis=1))
                if p < N_V_MXU:
                    vrow = v8[p][u:u + 1]
                    v_hi = _trunc_bf16(vrow)
                    lhs_v.append(jnp.concatenate(
                        [jnp.where(diag, v_hi, 0.0), jnp.where(diag, vrow - v_hi, 0.0)], axis=1))
            res = jnp.dot(jnp.concatenate(lhs_v + lhs_sa + lhs_y, axis=0), ones_bd, preferred_element_type=F32)
            res_v, res = res[:n_v], res[n_v:]
            for p in range(N_PAIRS):
                sa2 = res[RW_HEAD * p:RW_HEAD * (p + 1)]
                sa = sa2[:, :LANES] + sa2[:, LANES:]
                if p < N_V_MXU:
                    v2 = res_v[RW_HEAD * p:RW_HEAD * (p + 1)]
                    vcol = v2[:, :LANES] + v2[:, LANES:]
                else:
                    vrow = v8[p][u:u + 1]
                    va = jnp.sum(jnp.where(diag_a, vrow, 0.0), axis=1, keepdims=True)
                    vb = jnp.sum(jnp.where(diag_b, vrow, 0.0), axis=1, keepdims=True)
                    vcol = jnp.where(head_a, va, vb)
                s_scr[p] = s_scr[p] * w8[p][u:u + 1] + sa * b8[p][u:u + 1] + vcol * k8[p][u:u + 1]
            readout(res, ymask)
        return carry

    lax.fori_loop(0, tc // SUBLANES, step_group, 0)

    qs = [s_scr[p] * r_ref[pl.ds(tc - 1, 1), slice(LANES * p, LANES * (p + 1))] for p in range(N_PAIRS)]
    lhs_y = [jnp.concatenate(qs[p:p + 2], axis=1) for p in range(0, N_PAIRS, 2)]
    res = jnp.dot(jnp.concatenate(lhs_y, axis=0), ones_bd, preferred_element_type=F32)
    readout(jnp.concatenate([jnp.zeros((n_sa, 2 * LANES), F32), res], axis=0), (lane & (RW_HEAD - 1)) == (tc - 1))
    for p in range(N_PAIRS):
        yt = y_scr[p].T
        rows = jnp.concatenate([yt[:RW_HEAD], yt[RW_HEAD:]], axis=1)
        y_ref[:, LANES * p:LANES * (p + 1)] = rows[:tc]

    @pl.when(ci == pl.num_programs(1) - 1)
    def _():
        for p in range(N_PAIRS):
            s_p = s_scr[p]
            st_ref[2 * p] = s_p[:, :RW_HEAD]
            st_ref[2 * p + 1] = s_p[:, RW_HEAD:]


def _wkv_scan(r, w, k, v, kkn, b, s0, *, nb, t, tc):
    c = r.shape[1]
    h = c // RW_HEAD
    nc = t // tc
    row = pl.BlockSpec((tc, c), lambda bi, ci: (bi * nc + ci, 0))
    st = pl.BlockSpec((None, None, h, RW_HEAD, RW_HEAD), lambda bi, ci: (0, bi, 0, 0, 0))
    return pl.pallas_call(
        functools.partial(_wkv_kernel, tc=tc),
        grid=(nb, nc),
        in_specs=[row] * 6 + [st],
        out_specs=[row, st],
        out_shape=[jax.ShapeDtypeStruct((nb * t, c), F32), jax.ShapeDtypeStruct((1, nb, h, RW_HEAD, RW_HEAD), F32)],
        scratch_shapes=[pltpu.VMEM((N_PAIRS, RW_HEAD, LANES), F32), pltpu.VMEM((N_PAIRS, RW_HEAD, LANES), F32)],
        compiler_params=_params(("parallel", "arbitrary"), 48),
        name="wkv_scan",
    )(r, w, k, v, kkn, b, s0)


def _wkv_lanes_kernel(r_ref, w_ref, k_ref, v_ref, kkn_ref, b_ref, s0_ref, y_ref, st_ref, s_scr, *, steps):
    ci = pl.program_id(1)

    @pl.when(ci == 0)
    def _():
        s_scr[...] = s0_ref[...]

    def value_group(ig, carry):
        i0 = pl.multiple_of(ig * SUBLANES, SUBLANES)
        v8 = [v_ref[s, pl.ds(i0, SUBLANES), :] for s in range(steps)]
        ys = [[] for _ in range(steps)]
        for ii in range(SUBLANES):
            s_i = s_scr[i0 + ii]
            for s in range(steps):
                sa = jnp.sum(s_i * kkn_ref[s], axis=0, keepdims=True)
                s_i = s_i * w_ref[s] + sa * b_ref[s] + v8[s][ii:ii + 1] * k_ref[s]
                ys[s].append(jnp.sum(s_i * r_ref[s], axis=0, keepdims=True))
            s_scr[i0 + ii] = s_i
        for s in range(steps):
            y_ref[s, pl.ds(i0, SUBLANES), :] = jnp.concatenate(ys[s], axis=0)
        return carry

    lax.fori_loop(0, RW_HEAD // SUBLANES, value_group, 0)

    @pl.when(ci == pl.num_programs(1) - 1)
    def _():
        st_ref[...] = s_scr[...]


def _wkv_scan_lanes(vecs, s0, *, tc):
    t, c, nl = vecs[0].shape
    g = c // RW_HEAD
    vec = pl.BlockSpec((tc, RW_HEAD, nl), lambda gi, ci: (ci, gi, 0))
    st = pl.BlockSpec((None, RW_HEAD, RW_HEAD, nl), lambda gi, ci: (gi, 0, 0, 0))
    return pl.pallas_call(
        functools.partial(_wkv_lanes_kernel, steps=tc),
        grid=(g, t // tc),
        in_specs=[vec] * 6 + [st],
        out_specs=[vec, st],
        out_shape=[jax.ShapeDtypeStruct((t, c, nl), F32), jax.ShapeDtypeStruct((g, RW_HEAD, RW_HEAD, nl), F32)],
        scratch_shapes=[pltpu.VMEM((RW_HEAD, RW_HEAD, nl), F32)],
        compiler_params=_params(("parallel", "arbitrary"), 48),
        name="wkv_scan_lanes",
    )(*vecs, s0)


def _rw_post_kernel(y_ref, g_ref, bonus_ref, lw_ref, lb_ref, seg_ref, segt_ref, o_ref):
    y = y_ref[...]
    seg = seg_ref[...]
    seg_t = segt_ref[...]
    inv = 1.0 / RW_HEAD
    mu = _seg_sum_bcast(y, seg, seg_t) * inv
    d = y - mu
    var = _seg_sum_bcast(d * d, seg, seg_t) * inv
    yn = d * lax.rsqrt(var + GN_EPS)
    o_ref[...] = ((yn * lw_ref[...] + lb_ref[...] + bonus_ref[...]) * g_ref[...]).astype(o_ref.dtype)


def _rw_post(y, g, bonus, lnx_w, lnx_b, seg, seg_t, *, tm):
    m, c = y.shape
    blk = pl.BlockSpec((tm, c), lambda i: (i, 0))
    vec = pl.BlockSpec((1, c), lambda i: (0, 0))
    return pl.pallas_call(
        _rw_post_kernel,
        grid=(m // tm,),
        in_specs=[blk, blk, blk, vec, vec, pl.BlockSpec((c, LANES), lambda i: (0, 0)),
                  pl.BlockSpec((LANES, c), lambda i: (0, 0))],
        out_specs=blk,
        out_shape=jax.ShapeDtypeStruct((m, c), BF16),
        compiler_params=_params(("parallel",), 48),
        name="rw_post",
    )(y, g, bonus, lnx_w, lnx_b, seg, seg_t)


QB = 128


def _attn_prompt_kernel(sl_ref, q0, k0, v0, q1, k1, v1, q2, k2, v2, out_ref, o_scr, l_scr, *, t):
    h = pl.program_id(1)
    scale = ATT_HEAD_DIM ** -0.5
    iq = lax.broadcasted_iota(jnp.int32, (QB, QB), 0)
    ik = lax.broadcasted_iota(jnp.int32, (QB, QB), 1)
    d_cur = (iq - ik).astype(F32)
    refs = ((q0, k0, v0), (q1, k1, v1), (q2, k2, v2))
    for g, (win, dil) in enumerate(ATT_GROUPS):
        q_ref, k_ref, v_ref = refs[g]
        slope = sl_ref[g * ATT_HEADS + h]
        nblk = t // dil // QB
        bias_cur = jnp.where(ik <= iq, -slope * dil * d_cur, NEG_INF)
        bias_prev = jnp.where(ik >= iq, -slope * dil * (d_cur + QB), NEG_INF)

        def block(idx, carry, q_ref=q_ref, k_ref=k_ref, v_ref=v_ref, dil=dil, nblk=nblk, g=g,
                  bias_cur=bias_cur, bias_prev=bias_prev):
            res = idx // nblk
            qb = idx % nblk
            base = res + qb * (QB * dil)
            pbase = res + jnp.maximum(qb - 1, 0) * (QB * dil)
            rows = pl.ds(base, QB, stride=dil) if dil > 1 else pl.ds(base, QB)
            prows = pl.ds(pbase, QB, stride=dil) if dil > 1 else pl.ds(pbase, QB)
            q = q_ref[rows, :].astype(BF16)
            kc = k_ref[rows, :].astype(BF16)
            dims = (((1,), (1,)), ((), ()))
            s_c = lax.dot_general(q, kc, dims, preferred_element_type=F32) * scale + bias_cur
            m = jnp.max(s_c, axis=-1, keepdims=True)
            if nblk > 1:
                kp = k_ref[prows, :].astype(BF16)
                s_p = lax.dot_general(q, kp, dims, preferred_element_type=F32) * scale + jnp.where(
                    qb > 0, bias_prev, NEG_INF)
                m = jnp.maximum(m, jnp.max(s_p, axis=-1, keepdims=True))
            p_c = jnp.exp(s_c - m)
            l = jnp.sum(p_c, axis=-1, keepdims=True)
            o = jnp.dot(p_c.astype(BF16), v_ref[rows, :].astype(BF16), preferred_element_type=F32)
            if nblk > 1:
                p_p = jnp.exp(s_p - m)
                l = l + jnp.sum(p_p, axis=-1, keepdims=True)
                o = o + jnp.dot(p_p.astype(BF16), v_ref[prows, :].astype(BF16), preferred_element_type=F32)
            o_scr[g, rows, :] = o / l
            l_scr[g, rows, :] = jnp.broadcast_to(m + jnp.log(l), (QB, ATT_HEAD_DIM))
            return carry

        lax.fori_loop(0, dil * nblk, block, 0, unroll=4)

    l0, l1, l2 = l_scr[0], l_scr[1], l_scr[2]
    m = jnp.maximum(jnp.maximum(l0, l1), l2)
    e0, e1, e2 = jnp.exp(l0 - m), jnp.exp(l1 - m), jnp.exp(l2 - m)
    den = e0 + e1 + e2
    out_ref[...] = ((e0 * o_scr[0] + e1 * o_scr[1] + e2 * o_scr[2]) / den).astype(out_ref.dtype)


def _attn_prompt(z, slopes, *, nb, t):
    specs = []
    for g in range(len(ATT_GROUPS)):
        for c in range(3):
            cb = (g * 3 + c) * ATT_HEADS
            specs.append(pl.BlockSpec((t, ATT_HEAD_DIM), lambda b, h, sl, cb=cb: (b, cb + h)))
    grid_spec = pltpu.PrefetchScalarGridSpec(
        num_scalar_prefetch=1,
        grid=(nb, ATT_HEADS),
        in_specs=specs,
        out_specs=pl.BlockSpec((t, ATT_HEAD_DIM), lambda b, h, sl: (b, h)),
        scratch_shapes=[pltpu.VMEM((3, t, ATT_HEAD_DIM), F32), pltpu.VMEM((3, t, ATT_HEAD_DIM), F32)],
    )
    return pl.pallas_call(
        functools.partial(_attn_prompt_kernel, t=t),
        grid_spec=grid_spec,
        out_shape=jax.ShapeDtypeStruct((nb * t, ATT_HEADS * ATT_HEAD_DIM), BF16),
        compiler_params=_params(("parallel", "arbitrary"), 48),
        name="attn_prompt",
    )(slopes, *([z] * 9))


N_NEW = 8
N_BAND = 128


def _attn_sample_kernel(q0, q1, q2, n0, n1, n2, c0, c1, c2, sl_ref, out_ref, sc_scr, o_scr, l_scr):
    scale = ATT_HEAD_DIM ** -0.5 * LOG2_E
    qs, news, caches = (q0, q1, q2), (n0, n1, n2), (c0, c1, c2)
    tile = (ATT_HEADS, ATT_HEAD_DIM)

    def group(g):
        win, dil = ATT_GROUPS[g]
        slot = g % 2
        q_ref, n_ref, c_ref = qs[g], news[g], caches[g]
        slope = sl_ref[g] * LOG2_E
        past = N_BAND * dil
        qv = [q_ref[s] * scale for s in range(N_NEW)]
        if dil == 1:
            combos = [(0, s) for s in range(N_NEW)]
        elif dil == 4:
            combos = [(r, s) for r in range(4) for s in (r, r + 4)]
        else:
            combos = [(r, r) for r in range(N_NEW)]

        def row_tiles(m, r):
            if dil == 1:
                return c_ref[m, 0], c_ref[m, 1]
            if dil == 4:
                a = m * 4 + r
                return c_ref[a, 0], c_ref[a, 1]
            return c_ref[m, r, 0], c_ref[m, r, 1]

        def score(m, r, s):
            a = m * dil + r
            k_t, _ = row_tiles(m, r)
            sc = jnp.sum(qv[s] * k_t, axis=-1, keepdims=True) - slope * jnp.asarray(past + s - a, F32)
            if dil == 1:
                sc = jnp.where(m >= s, sc, NEG_INF)
            elif dil == 4 and s >= 4:
                sc = jnp.where(m >= 1, sc, NEG_INF)
            return jnp.broadcast_to(sc, tile)

        new_sc = []
        for s in range(N_NEW):
            lst = []
            for j in range(s // dil + 1):
                s2 = s - j * dil
                sc = jnp.sum(qv[s] * n_ref[s2, 0], axis=-1, keepdims=True) - slope * float(j * dil)
                lst.append((s2, jnp.broadcast_to(sc, tile)))
            new_sc.append(lst)
        m0 = [functools.reduce(jnp.maximum, [x for _, x in new_sc[s]]) for s in range(N_NEW)]

        def pass1(m, mx):
            mx = list(mx)
            for ci, (r, s) in enumerate(combos):
                sc = score(m, r, s)
                sc_scr[slot, m, ci] = sc
                mx[s] = jnp.maximum(mx[s], sc)
            return tuple(mx)

        def start2(mx):
            l_init, o_init = [], []
            for s in range(N_NEW):
                l_s = jnp.zeros(tile, F32)
                o_s = jnp.zeros(tile, F32)
                for s2, sc in new_sc[s]:
                    p = jnp.exp2(sc - mx[s])
                    l_s = l_s + p
                    o_s = o_s + p * n_ref[s2, 1]
                l_init.append(l_s)
                o_init.append(o_s)
            return tuple(l_init), tuple(o_init)

        def pass2(m, carry, mx):
            ls, os_ = list(carry[0]), list(carry[1])
            for ci, (r, s) in enumerate(combos):
                _, v_t = row_tiles(m, r)
                p = jnp.exp2(sc_scr[slot, m, ci] - mx[s])
                ls[s] = ls[s] + p
                os_[s] = os_[s] + p * v_t
            return tuple(ls), tuple(os_)

        def finish(carry, mx):
            ls, os_ = carry
            for s in range(N_NEW):
                o_scr[g, s] = os_[s] / ls[s]
                l_scr[g, s] = mx[s] * (1.0 / LOG2_E) + jnp.log(ls[s])

        return tuple(m0), pass1, start2, pass2, finish

    m0, pass1, start2, pass2, finish = group(0)
    mx_prev = lax.fori_loop(0, N_BAND, pass1, m0, unroll=32)
    for g in range(1, len(ATT_GROUPS)):
        m0, pass1, start2_next, pass2_next, finish_next = group(g)

        def fused(m, carry, pass1=pass1, pass2=pass2, mx_prev=mx_prev):
            return pass1(m, carry[0]), pass2(m, carry[1], mx_prev)

        mx_new, acc = lax.fori_loop(0, N_BAND, fused, (m0, start2(mx_prev)), unroll=16)
        finish(acc, mx_prev)
        mx_prev, start2, pass2, finish = mx_new, start2_next, pass2_next, finish_next
    acc = lax.fori_loop(0, N_BAND, lambda m, carry: pass2(m, carry, mx_prev), start2(mx_prev), unroll=8)
    finish(acc, mx_prev)

    for s in range(N_NEW):
        l0, l1, l2 = l_scr[0, s], l_scr[1, s], l_scr[2, s]
        m = jnp.maximum(jnp.maximum(l0, l1), l2)
        e0, e1, e2 = jnp.exp(l0 - m), jnp.exp(l1 - m), jnp.exp(l2 - m)
        out_ref[s] = (e0 * o_scr[0, s] + e1 * o_scr[1, s] + e2 * o_scr[2, s]) / (e0 + e1 + e2)


def _attn_sample(qs, news, caches, slope_tiles):
    nb = qs[0].shape[0]
    he = (ATT_HEADS, ATT_HEAD_DIM)
    qspec = pl.BlockSpec((None, N_NEW) + he, lambda b: (b, 0, 0, 0))
    nspec = pl.BlockSpec((None, N_NEW, 2) + he, lambda b: (b, 0, 0, 0, 0))
    c2 = caches[2]
    dil2 = ATT_GROUPS[2][1]
    c2v = c2.reshape(nb, c2.shape[1] // dil2, dil2, 2, *he)
    cspecs = [
        pl.BlockSpec((None, caches[0].shape[1], 2) + he, lambda b: (b, 0, 0, 0, 0)),
        pl.BlockSpec((None, caches[1].shape[1], 2) + he, lambda b: (b, 0, 0, 0, 0)),
        pl.BlockSpec((None, N_BAND, N_NEW, 2) + he, lambda b: (b, 0, 0, 0, 0, 0)),
    ]
    return pl.pallas_call(
        _attn_sample_kernel,
        grid=(nb,),
        in_specs=[qspec] * 3 + [nspec] * 3 + cspecs + [pl.BlockSpec((3,) + he, lambda b: (0, 0, 0))],
        out_specs=pl.BlockSpec((None, N_NEW) + he, lambda b: (b, 0, 0, 0)),
        out_shape=jax.ShapeDtypeStruct((nb, N_NEW) + he, F32),
        scratch_shapes=[pltpu.VMEM((2, N_BAND, N_NEW) + he, F32), pltpu.VMEM((3, N_NEW) + he, F32),
                        pltpu.VMEM((3, N_NEW) + he, F32)],
        compiler_params=_params(("parallel",), 48),
        name="attn_sample",
    )(*qs, *news, caches[0], caches[1], c2v, slope_tiles)


def kernel(x_prompt, x_sample, cache_kv_0, cache_kv_1, cache_kv_2, state_wkv, state_shift, norm_mix, w_in, rw_mu,
           rw_w0, rw_w_up, rw_a0, rw_a_up, rw_g_up, rw_k_k, rw_k_a, rw_r_k, rw_lnx_w, rw_lnx_b, rw_out, att_out,
           w_o, norm_ffn, w_ffn_in, w_ffn_out, norm_final):
    depth = w_in.shape[0]
    assert depth == 1, "single-layer stack"
    bp, tp, d = x_prompt.shape
    bs, ts, _ = x_sample.shape
    mp, ms = bp * tp, bs * ts
    assert ts == N_NEW
    c = rw_w0.shape[1]
    r_dec, r_aaa, r_gate = rw_w_up.shape[1], rw_a_up.shape[1], rw_g_up.shape[1]
    rw_cols = 3 * c + r_dec + r_aaa + r_gate
    att_cols = len(ATT_GROUPS) * 3 * ATT_HEADS * ATT_HEAD_DIM
    att_w = ATT_HEADS * ATT_HEAD_DIM
    pad = (-(att_cols + rw_cols)) % LANES
    rwp = rw_cols + pad
    assert rwp - 3 * c == 512
    nz = att_cols + rwp

    tail = lambda w, off: jnp.zeros((512, c), F32).at[off:off + w.shape[0]].set(w).astype(BF16)
    row = lambda v, n: jnp.pad(v.reshape(1, -1), ((0, 0), (0, n - v.size)))
    seg = (jnp.arange(c // 2)[:, None] // RW_HEAD == jnp.arange(LANES)[None, :]).astype(BF16)
    prm = {
        "mu": row(rw_mu[0], rwp), "w0": row(rw_w0[0], c), "a0": row(rw_a0[0], c),
        "wup": tail(rw_w_up[0], 0), "aup": tail(rw_a_up[0], r_dec), "gup": tail(rw_g_up[0], r_dec + r_aaa),
        "k_k": row(rw_k_k[0], c), "k_a": row(rw_k_a[0], c), "r_k": row(rw_r_k[0].reshape(-1), c),
        "seg": seg, "seg_t": seg.T,
    }
    seg_full = (jnp.arange(c)[:, None] // RW_HEAD == jnp.arange(LANES)[None, :]).astype(BF16)

    x = jnp.concatenate([x_prompt.reshape(mp, d), x_sample.reshape(ms, d)], axis=0)
    m_all = mp + ms
    hmix = _rmsnorm(x, norm_mix[0], BF16)
    wi_t = jnp.swapaxes(w_in, 1, 2)[0]
    z = _matmul(hmix, wi_t, tm=1024, tn=512, n_out=nz, transposed=True, vmem_mib=56, name="w_in")
    zg = _matmul(hmix, wi_t[att_cols + rw_cols:], tm=1024, tn=512, transposed=True, vmem_mib=56,
                 name="w_gate")
    z_s = z[mp:].reshape(bs, ts, nz)

    new_kv_p, new_kv_s = [], []
    for g, (win, dil) in enumerate(ATT_GROUPS):
        lo = g * 3 * att_w + att_w
        keep = min(win, tp)
        kv = jnp.stack([z[(b + 1) * tp - keep:(b + 1) * tp, lo:lo + 2 * att_w] for b in range(bp)])
        new_kv_p.append(kv.reshape(bp, keep, 2, ATT_HEADS, ATT_HEAD_DIM)[None])
        new_kv_s.append(z_s[:, :, lo:lo + 2 * att_w].reshape(bs, ts, 2, ATT_HEADS, ATT_HEAD_DIM)[None])
    shift_p = jnp.stack([z[(b + 1) * tp - 1, att_cols:att_cols + rw_cols] for b in range(bp)])[None]
    shift_s = z_s[:, -1, att_cols:att_cols + rw_cols][None]

    tm_prep = 256
    prep_p = _rwkv_prep(z, 0, mp, att_cols, None, jnp.zeros((bp, 1, rwp), F32), tp, prm, tm=tm_prep)
    cols_s = z_s[:, :, att_cols:att_cols + rwp]
    shift_in = jnp.pad(state_shift[0], ((0, 0), (0, pad)))
    prev_s = jnp.concatenate([shift_in[:, None, :], cols_s[:, :-1]], axis=1).reshape(ms, rwp)
    prep_s = _rwkv_prep(z, mp, ms, att_cols, prev_s, None, ts, prm, tm=tm_prep)

    h_rw = c // RW_HEAD
    y_p, wkv_p = _wkv_scan(*prep_p[:6], jnp.zeros((1, bp, h_rw, RW_HEAD, RW_HEAD), F32), nb=bp, t=tp, tc=64)
    to_lanes_s = lambda a: a.reshape(bs, ts, c).transpose(1, 2, 0)
    y_t, wkv_t = _wkv_scan_lanes([to_lanes_s(a) for a in prep_s[:6]], jnp.transpose(state_wkv[0], (1, 2, 3, 0)),
                                 tc=ts)
    y_s = y_t.transpose(2, 0, 1).reshape(ms, c)
    wkv_s = jnp.transpose(wkv_t, (3, 0, 1, 2))[None]
    lw, lb = rw_lnx_w[0].reshape(1, c), rw_lnx_b[0].reshape(1, c)
    rw_p = _rw_post(y_p, prep_p[6], prep_p[7], lw, lb, seg_full, seg_full.T, tm=256)
    rw_s = _rw_post(y_s, prep_s[6], prep_s[7], lw, lb, seg_full, seg_full.T, tm=256)
    rw_act = jnp.concatenate([rw_p, rw_s], axis=0)

    n_heads_all = len(ATT_GROUPS) * ATT_HEADS
    slopes = 2.0 ** (-8.0 * jnp.arange(1, n_heads_all + 1, dtype=F32) / n_heads_all)
    att_p = _attn_prompt(z, slopes, nb=bp, t=tp)
    qs = [z_s[:, :, g * 3 * att_w:g * 3 * att_w + att_w].reshape(bs, ts, ATT_HEADS, ATT_HEAD_DIM)
          for g in range(len(ATT_GROUPS))]
    slope_tiles = jnp.broadcast_to(slopes.reshape(len(ATT_GROUPS), ATT_HEADS, 1),
                                   (len(ATT_GROUPS), ATT_HEADS, ATT_HEAD_DIM))
    att_s = _attn_sample(qs, [kv[0] for kv in new_kv_s], [cache_kv_0[0], cache_kv_1[0], cache_kv_2[0]], slope_tiles)
    att_act = jnp.concatenate([att_p, att_s.reshape(ms, att_w).astype(BF16)], axis=0)

    mixed = _mix(rw_act, att_act, rw_out[0].astype(BF16), att_out[0].astype(BF16), zg, 0, tm=1024, tn=512)
    x1 = _matmul(mixed, w_o[0].astype(BF16), tm=1024, tn=512, residual=x, name="w_o")
    hf = _rmsnorm(x1, norm_ffn[0], BF16)
    act = _swiglu(hf, w_ffn_in[0], tm=1024, tn=256)
    x2 = _matmul(act, w_ffn_out[0].astype(BF16), tm=512, tn=256, residual=x1, vmem_mib=56, name="w_ffn_out")
    y_p = _rmsnorm(x2, norm_final, F32, 0, mp)
    y_s = _rmsnorm(x2, norm_final, F32, mp, ms)

    return (y_p.reshape(bp, tp, d), y_s.reshape(bs, ts, d),
            new_kv_p[0], new_kv_p[1], new_kv_p[2], wkv_p, shift_p,
            new_kv_s[0], new_kv_s[1], new_kv_s[2], wkv_s, shift_s)
```

```python
import functools

import jax
import jax.numpy as jnp
from jax import lax
from jax.experimental import pallas as pl
from jax.experimental.pallas import tpu as pltpu

F32 = jnp.float32
BF16 = jnp.bfloat16

RMS_EPS = 1e-6
GN_EPS = 64e-5
RW_HEAD = 64
ATT_GROUPS = ((128, 1), (512, 4), (2048, 16))
ATT_HEAD_DIM = 128
ATT_HEADS = 8
LANES = 128
SUBLANES = 8
NEG_INF = float("-inf")
LOG2_E = 1.4426950408889634


def _params(sem, vmem_mib):
    return pltpu.CompilerParams(dimension_semantics=sem, vmem_limit_bytes=vmem_mib << 20)


def _rmsnorm_kernel(x_ref, g_ref, o_ref):
    x = x_ref[...]
    ms = jnp.mean(x * x, axis=-1, keepdims=True)
    o_ref[...] = (x * lax.rsqrt(ms + RMS_EPS) * g_ref[...]).astype(o_ref.dtype)


def _rmsnorm(x, g, out_dtype, row0=0, rows=None, tm=256):
    d = x.shape[1]
    rows = x.shape[0] if rows is None else rows
    rb0 = row0 // tm
    return pl.pallas_call(
        _rmsnorm_kernel,
        grid=(rows // tm,),
        in_specs=[pl.BlockSpec((tm, d), lambda i: (rb0 + i, 0)), pl.BlockSpec((1, d), lambda i: (0, 0))],
        out_specs=pl.BlockSpec((tm, d), lambda i: (i, 0)),
        out_shape=jax.ShapeDtypeStruct((rows, d), out_dtype),
        compiler_params=_params(("parallel",), 32),
        name="rmsnorm",
    )(x, g.reshape(1, d))


def _rmsnorm_pair_kernel(xa_ref, xb_ref, g_ref, o_ref, *, na):
    x = jnp.where(pl.program_id(0) < na, xa_ref[...], xb_ref[...])
    ms = jnp.mean(x * x, axis=-1, keepdims=True)
    o_ref[...] = (x * lax.rsqrt(ms + RMS_EPS) * g_ref[...]).astype(o_ref.dtype)


def _rmsnorm_pair(xa, xb, g, out_dtype, tm=256):
    d = xa.shape[1]
    na, nb = xa.shape[0] // tm, xb.shape[0] // tm
    return pl.pallas_call(
        functools.partial(_rmsnorm_pair_kernel, na=na),
        grid=(na + nb,),
        in_specs=[pl.BlockSpec((tm, d), lambda i: (jnp.minimum(i, na - 1), 0)),
                  pl.BlockSpec((tm, d), lambda i: (jnp.maximum(i - na, 0), 0)),
                  pl.BlockSpec((1, d), lambda i: (0, 0))],
        out_specs=pl.BlockSpec((tm, d), lambda i: (i, 0)),
        out_shape=jax.ShapeDtypeStruct(((na + nb) * tm, d), out_dtype),
        compiler_params=_params(("parallel",), 32),
        name="rmsnorm_pair",
    )(xa, xb, g.reshape(1, d))


def _mm_kernel(a_ref, w_ref, o_ref):
    o_ref[...] = jnp.dot(a_ref[...], w_ref[...].astype(BF16), preferred_element_type=F32).astype(o_ref.dtype)


def _mm_res_kernel(a_ref, w_ref, r_ref, o_ref):
    o_ref[...] = r_ref[...] + jnp.dot(a_ref[...], w_ref[...].astype(BF16), preferred_element_type=F32)


def _mm_res_pair_kernel(a_ref, w_ref, ra_ref, rb_ref, o_ref, *, na):
    res = jnp.where(pl.program_id(0) < na, ra_ref[...], rb_ref[...])
    o_ref[...] = res + jnp.dot(a_ref[...], w_ref[...].astype(BF16), preferred_element_type=F32)


def _mm_nt_kernel(a_ref, wt_ref, o_ref):
    o_ref[...] = lax.dot_general(a_ref[...], wt_ref[...].astype(BF16), (((1,), (1,)), ((), ())),
                                 preferred_element_type=F32)


def _matmul(a, w, *, tm, tn, n_out=None, residual=None, transposed=False, vmem_mib=48, name="matmul"):
    m, k = a.shape
    if transposed:
        assert residual is None
        n = w.shape[0] if n_out is None else n_out
        w_spec = pl.BlockSpec((tn, k), lambda i, j: (j, 0))
        kern = _mm_nt_kernel
    else:
        n = w.shape[1] if n_out is None else n_out
        w_spec = pl.BlockSpec((k, tn), lambda i, j: (0, j))
        kern = _mm_kernel
    in_specs = [pl.BlockSpec((tm, k), lambda i, j: (i, 0)), w_spec]
    args = [a, w]
    if isinstance(residual, tuple):
        ra, rb = residual
        na = ra.shape[0] // tm
        in_specs.append(pl.BlockSpec((tm, tn), lambda i, j: (jnp.minimum(i, na - 1), jnp.where(i < na, j, 0))))
        in_specs.append(pl.BlockSpec((tm, tn), lambda i, j: (jnp.maximum(i - na, 0), jnp.where(i < na, 0, j))))
        args += [ra, rb]
        kern = functools.partial(_mm_res_pair_kernel, na=na)
    elif residual is not None:
        in_specs.append(pl.BlockSpec((tm, tn), lambda i, j: (i, j)))
        args.append(residual)
        kern = _mm_res_kernel
    return pl.pallas_call(
        kern,
        grid=(m // tm, n // tn),
        in_specs=in_specs,
        out_specs=pl.BlockSpec((tm, tn), lambda i, j: (i, j)),
        out_shape=jax.ShapeDtypeStruct((m, n), F32),
        compiler_params=_params(("parallel", "arbitrary"), vmem_mib),
        name=name,
    )(*args)


def _swiglu_kernel(a_ref, wg_ref, wu_ref, o_ref):
    a = a_ref[...]
    g = jnp.dot(a, wg_ref[...].astype(BF16), preferred_element_type=F32)
    u = jnp.dot(a, wu_ref[...].astype(BF16), preferred_element_type=F32)
    o_ref[...] = (g * jax.nn.sigmoid(g) * u).astype(o_ref.dtype)


def _swiglu(a, w, *, tm, tn):
    m, k = a.shape
    f = w.shape[1] // 2
    nb = f // tn
    return pl.pallas_call(
        _swiglu_kernel,
        grid=(m // tm, nb),
        in_specs=[
            pl.BlockSpec((tm, k), lambda i, j: (i, 0)),
            pl.BlockSpec((k, tn), lambda i, j: (0, j)),
            pl.BlockSpec((k, tn), lambda i, j: (0, j + nb)),
        ],
        out_specs=pl.BlockSpec((tm, tn), lambda i, j: (i, j)),
        out_shape=jax.ShapeDtypeStruct((m, f), BF16),
        compiler_params=_params(("parallel", "arbitrary"), 56),
        name="swiglu",
    )(a, w, w)


def _mix_kernel(rw_ref, at_ref, wr_ref, wa_ref, ga_ref, gb_ref, o_ref):
    rw = jnp.dot(rw_ref[...], wr_ref[...], preferred_element_type=F32)
    at = jnp.dot(at_ref[...], wa_ref[...], preferred_element_type=F32)
    mixed = jax.nn.sigmoid(ga_ref[...]) * rw + jax.nn.sigmoid(gb_ref[...]) * at
    o_ref[...] = mixed.astype(o_ref.dtype)


def _mix(rw, at, w_rw, w_at, z, gate_col0, *, tm, tn):
    m, kr = rw.shape
    ka = at.shape[1]
    d = w_rw.shape[1]
    nb = d // tn
    g0 = gate_col0 // tn
    return pl.pallas_call(
        _mix_kernel,
        grid=(m // tm, nb),
        in_specs=[
            pl.BlockSpec((tm, kr), lambda i, j: (i, 0)),
            pl.BlockSpec((tm, ka), lambda i, j: (i, 0)),
            pl.BlockSpec((kr, tn), lambda i, j: (0, j)),
            pl.BlockSpec((ka, tn), lambda i, j: (0, j)),
            pl.BlockSpec((tm, tn), lambda i, j: (i, g0 + j)),
            pl.BlockSpec((tm, tn), lambda i, j: (i, g0 + nb + j)),
        ],
        out_specs=pl.BlockSpec((tm, tn), lambda i, j: (i, j)),
        out_shape=jax.ShapeDtypeStruct((m, d), BF16),
        compiler_params=_params(("parallel", "arbitrary"), 48),
        name="mix",
    )(rw, at, w_rw, w_at, z, z)


def _split_bf16(x):
    hi = x.astype(BF16)
    lo = (x - hi.astype(F32)).astype(BF16)
    return hi, lo


def _seg_sum_bcast(x, seg, seg_t):
    hi, lo = _split_bf16(x)
    s = jnp.dot(hi, seg, preferred_element_type=F32) + jnp.dot(lo, seg, preferred_element_type=F32)
    shi, slo = _split_bf16(s)
    return jnp.dot(shi, seg_t, preferred_element_type=F32) + jnp.dot(slo, seg_t, preferred_element_type=F32)


def _prep_body(cur, prev, mu, w0, wup, a0, aup, gup, k_k, k_a, r_k, seg, seg_t, outs):
    r_ref, w_ref, k_ref, v_ref, kkn_ref, b_ref, g_ref, bonus_ref = outs
    xs = {n: cur[n] + (prev[n] - cur[n]) * mu[n] for n in ("r", "k", "v", "t")}
    r, k, v, t = xs["r"], xs["k"], xs["v"], xs["t"]
    wl = w0 + jnp.dot(jnp.tanh(t).astype(BF16), wup, preferred_element_type=F32)
    w_log = -(jnp.maximum(-wl, 0.0) + jnp.log(1.0 + jnp.exp(-jnp.abs(wl)))) - 0.5
    decay = jnp.exp(-jnp.exp(w_log))
    a = jax.nn.sigmoid(a0 + jnp.dot(t.astype(BF16), aup, preferred_element_type=F32))
    g = jnp.dot(jax.nn.sigmoid(t).astype(BF16), gup, preferred_element_type=F32)
    kk = k * k_k
    kk = kk * lax.rsqrt(jnp.maximum(_seg_sum_bcast(kk * kk, seg, seg_t), 1e-24))
    k_mod = k * (1.0 + (a - 1.0) * k_a)
    bonus = _seg_sum_bcast(r * k_mod * r_k, seg, seg_t) * v
    r_ref[...] = r
    w_ref[...] = decay
    k_ref[...] = k_mod
    v_ref[...] = v
    kkn_ref[...] = -kk
    b_ref[...] = kk * a
    g_ref[...] = g
    bonus_ref[...] = bonus


def _prep_roll_kernel(zr, zk, zv, zt, br, bk, bv, bt, sr, sk, sv, st, mr, mk, mv, mt, w0, wup, a0, aup, gup, k_k,
                      k_a, r_k, seg, seg_t, *outs, tiles_per_seq):
    cur = {"r": zr[...], "k": zk[...], "v": zv[...], "t": zt[...]}
    seq_start = pl.program_id(0) % tiles_per_seq == 0
    prev = {}
    for n, bref, sref in (("r", br, sr), ("k", bk, sk), ("v", bv, sv), ("t", bt, st)):
        x = cur[n]
        rolled = pltpu.roll(x, 1, 0)
        row0 = lax.broadcasted_iota(jnp.int32, x.shape, 0) == 0
        first = jnp.where(seq_start, sref[...], bref[SUBLANES - 1:, :])
        prev[n] = jnp.where(row0, first, rolled)
    mu = {"r": mr[...], "k": mk[...], "v": mv[...], "t": mt[...]}
    _prep_body(cur, prev, mu, w0[...], wup[...], a0[...], aup[...], gup[...], k_k[...], k_a[...], r_k[...],
               seg[...], seg_t[...], outs)


def _prep_given_kernel(zr, zk, zv, zt, pr, pk, pv, pt, mr, mk, mv, mt, w0, wup, a0, aup, gup, k_k, k_a, r_k, seg,
                       seg_t, *outs):
    cur = {"r": zr[...], "k": zk[...], "v": zv[...], "t": zt[...]}
    prev = {"r": pr[...], "k": pk[...], "v": pv[...], "t": pt[...]}
    mu = {"r": mr[...], "k": mk[...], "v": mv[...], "t": mt[...]}
    _prep_body(cur, prev, mu, w0[...], wup[...], a0[...], aup[...], gup[...], k_k[...], k_a[...], r_k[...],
               seg[...], seg_t[...], outs)


def _rwkv_prep(z, row0, rows, rw_col0, prev_arr, shift_arr, seq_len, p, *, tm):
    c = p["w0"].shape[1]
    hc = c // 2
    nrb = rows // tm
    rb0 = row0 // tm
    cb = rw_col0 // hc
    tb = (rw_col0 + 3 * c) // 512
    zspec = lambda off: pl.BlockSpec((tm, hc), lambda i, j: (rb0 + i, cb + off + j))
    in_specs = [zspec(0), zspec(2), zspec(4), pl.BlockSpec((tm, 512), lambda i, j: (rb0 + i, tb))]
    args = [z, z, z, z]
    if prev_arr is not None:
        kern = _prep_given_kernel
        pspec = lambda off: pl.BlockSpec((tm, hc), lambda i, j: (i, off + j))
        in_specs += [pspec(0), pspec(2), pspec(4), pl.BlockSpec((tm, 512), lambda i, j: (i, 3 * c // 512))]
        args += [prev_arr] * 4
    else:
        tiles_per_seq = seq_len // tm
        kern = functools.partial(_prep_roll_kernel, tiles_per_seq=tiles_per_seq)
        before = lambda i: jnp.maximum((rb0 + i) * (tm // SUBLANES) - 1, 0)
        bspec = lambda off: pl.BlockSpec((SUBLANES, hc), lambda i, j: (before(i), cb + off + j))
        in_specs += [bspec(0), bspec(2), bspec(4), pl.BlockSpec((SUBLANES, 512), lambda i, j: (before(i), tb))]
        args += [z] * 4
        sspec = lambda off: pl.BlockSpec((None, 1, hc), lambda i, j: (i // tiles_per_seq, 0, off + j))
        in_specs += [sspec(0), sspec(2), sspec(4),
                     pl.BlockSpec((None, 1, 512), lambda i, j: (i // tiles_per_seq, 0, 3 * c // 512))]
        args += [shift_arr] * 4
    mspec = lambda off: pl.BlockSpec((1, hc), lambda i, j: (0, off + j))
    in_specs += [mspec(0), mspec(2), mspec(4), pl.BlockSpec((1, 512), lambda i, j: (0, 3 * c // 512))]
    args += [p["mu"]] * 4
    vec = pl.BlockSpec((1, hc), lambda i, j: (0, j))
    lowrank = pl.BlockSpec((512, hc), lambda i, j: (0, j))
    in_specs += [vec, lowrank, vec, lowrank, lowrank, vec, vec, vec,
                 pl.BlockSpec((hc, LANES), lambda i, j: (0, 0)), pl.BlockSpec((LANES, hc), lambda i, j: (0, 0))]
    args += [p["w0"], p["wup"], p["a0"], p["aup"], p["gup"], p["k_k"], p["k_a"], p["r_k"], p["seg"], p["seg_t"]]
    out_spec = pl.BlockSpec((tm, hc), lambda i, j: (i, j))
    return pl.pallas_call(
        kern,
        grid=(nrb, 2),
        in_specs=in_specs,
        out_specs=[out_spec] * 8,
        out_shape=[jax.ShapeDtypeStruct((rows, c), F32)] * 8,
        compiler_params=_params(("parallel", "arbitrary"), 48),
        name="rwkv_prep",
    )(*args)


N_PAIRS = 16
N_V_MXU = 0


def _trunc_bf16(x):
    return lax.bitcast_convert_type(lax.bitcast_convert_type(x, jnp.int32) & jnp.int32(-65536), F32)


def _wkv_kernel(r_ref, w_ref, k_ref, v_ref, kkn_ref, b_ref, s0_ref, y_ref, st_ref, s_scr, y_scr, *, tc):
    ci = pl.program_id(1)

    @pl.when(ci == 0)
    def _():
        for p in range(N_PAIRS):
            s_scr[p] = jnp.concatenate([s0_ref[2 * p], s0_ref[2 * p + 1]], axis=1)

    y_scr[...] = jnp.zeros(y_scr.shape, F32)
    lane = lax.broadcasted_iota(jnp.int32, (RW_HEAD, LANES), 1)
    sub = lax.broadcasted_iota(jnp.int32, (RW_HEAD, LANES), 0)
    head_a = lane < RW_HEAD
    diag_a = lane == sub
    diag_b = lane == sub + RW_HEAD
    diag = (lane & (RW_HEAD - 1)) == sub
    n_v = N_V_MXU * RW_HEAD
    wr = lax.broadcasted_iota(jnp.int32, (2 * LANES, 2 * LANES), 0)
    wc = lax.broadcasted_iota(jnp.int32, (2 * LANES, 2 * LANES), 1)
    ones_bd = ((wr // RW_HEAD) == (wc // RW_HEAD)).astype(F32)
    n_sa = N_PAIRS * RW_HEAD

    def readout(res, ymask):
        for p in range(N_PAIRS):
            blk = n_sa + RW_HEAD * (p // 2)
            yb = res[blk:blk + RW_HEAD, LANES * (p % 2):LANES * (p % 2 + 1)]
            y_scr[p] = jnp.where(ymask, yb, y_scr[p])

    def step_group(gi, carry):
        t0 = pl.multiple_of(gi * SUBLANES, SUBLANES)
        tp0 = pl.multiple_of(jnp.maximum(gi - 1, 0) * SUBLANES, SUBLANES)
        rows8 = lambda ref, p: ref[pl.ds(t0, SUBLANES), LANES * p:LANES * (p + 1)]
        kkn8 = [rows8(kkn_ref, p) for p in range(N_PAIRS)]
        r8 = [rows8(r_ref, p) for p in range(N_PAIRS)]
        v8 = [rows8(v_ref, p) for p in range(N_PAIRS)]
        w8 = [rows8(w_ref, p) for p in range(N_PAIRS)]
        b8 = [rows8(b_ref, p) for p in range(N_PAIRS)]
        k8 = [rows8(k_ref, p) for p in range(N_PAIRS)]
        r_last = [r_ref[pl.ds(tp0, SUBLANES), LANES * p:LANES * (p + 1)][SUBLANES - 1:] for p in range(N_PAIRS)]
        for u in range(SUBLANES):
            ymask = (lane & (RW_HEAD - 1)) == (t0 + (u - 1))
            lhs_sa, lhs_y, lhs_v, qs = [], [], [], []
            for p in range(N_PAIRS):
                s_p = s_scr[p]
                pr = s_p * kkn8[p][u:u + 1]
                pr_hi = _trunc_bf16(pr)
                lhs_sa.append(jnp.concatenate([pr_hi, pr - pr_hi], axis=1))
                qs.append(s_p * (r8[p][u - 1:u] if u else r_last[p]))
                if p % 2:
                    lhs_y.append(jnp.concatenate(qs[-2:], axis=1))
                if p < N_V_MXU:
                    vrow = v8[p][u:u + 1]
                    v_hi = _trunc_bf16(vrow)
                    lhs_v.append(jnp.concatenate(
                        [jnp.where(diag, v_hi, 0.0), jnp.where(diag, vrow - v_hi, 0.0)], axis=1))
            res = jnp.dot(jnp.concatenate(lhs_v + lhs_sa + lhs_y, axis=0), ones_bd, preferred_element_type=F32)
            res_v, res = res[:n_v], res[n_v:]
            for p in range(N_PAIRS):
                sa2 = res[RW_HEAD * p:RW_HEAD * (p + 1)]
                sa = sa2[:, :LANES] + sa2[:, LANES:]
                if p < N_V_MXU:
                    v2 = res_v[RW_HEAD * p:RW_HEAD * (p + 1)]
                    vcol = v2[:, :LANES] + v2[:, LANES:]
                else:
                    vrow = v8[p][u:u + 1]
                    va = jnp.sum(jnp.where(diag_a, vrow, 0.0), axis=1, keepdims=True)
                    vb = jnp.sum(jnp.where(diag_b, vrow, 0.0), axis=1, keepdims=True)
                    vcol = jnp.where(head_a, va, vb)
                s_scr[p] = s_scr[p] * w8[p][u:u + 1] + sa * b8[p][u:u + 1] + vcol * k8[p][u:u + 1]
            readout(res, ymask)
        return carry

    lax.fori_loop(0, tc // SUBLANES, step_group, 0)

    qs = [s_scr[p] * r_ref[pl.ds(tc - 1, 1), slice(LANES * p, LANES * (p + 1))] for p in range(N_PAIRS)]
    lhs_y = [jnp.concatenate(qs[p:p + 2], axis=1) for p in range(0, N_PAIRS, 2)]
    res = jnp.dot(jnp.concatenate(lhs_y, axis=0), ones_bd, preferred_element_type=F32)
    readout(jnp.concatenate([jnp.zeros((n_sa, 2 * LANES), F32), res], axis=0), (lane & (RW_HEAD - 1)) == (tc - 1))
    for p in range(N_PAIRS):
        yt = y_scr[p].T
        rows = jnp.concatenate([yt[:RW_HEAD], yt[RW_HEAD:]], axis=1)
        y_ref[:, LANES * p:LANES * (p + 1)] = rows[:tc]

    @pl.when(ci == pl.num_programs(1) - 1)
    def _():
        for p in range(N_PAIRS):
            s_p = s_scr[p]
            st_ref[2 * p] = s_p[:, :RW_HEAD]
            st_ref[2 * p + 1] = s_p[:, RW_HEAD:]


def _wkv_scan(r, w, k, v, kkn, b, s0, *, nb, t, tc):
    c = r.shape[1]
    h = c // RW_HEAD
    nc = t // tc
    row = pl.BlockSpec((tc, c), lambda bi, ci: (bi * nc + ci, 0))
    st = pl.BlockSpec((None, None, h, RW_HEAD, RW_HEAD), lambda bi, ci: (0, bi, 0, 0, 0))
    return pl.pallas_call(
        functools.partial(_wkv_kernel, tc=tc),
        grid=(nb, nc),
        in_specs=[row] * 6 + [st],
        out_specs=[row, st],
        out_shape=[jax.ShapeDtypeStruct((nb * t, c), F32), jax.ShapeDtypeStruct((1, nb, h, RW_HEAD, RW_HEAD), F32)],
        scratch_shapes=[pltpu.VMEM((N_PAIRS, RW_HEAD, LANES), F32), pltpu.VMEM((N_PAIRS, RW_HEAD, LANES), F32)],
        compiler_params=_params(("parallel", "arbitrary"), 48),
        name="wkv_scan",
    )(r, w, k, v, kkn, b, s0)


def _wkv_lanes_kernel(r_ref, w_ref, k_ref, v_ref, kkn_ref, b_ref, s0_ref, y_ref, st_ref, s_scr, *, steps):
    ci = pl.program_id(1)

    @pl.when(ci == 0)
    def _():
        s_scr[...] = s0_ref[...]

    def value_group(ig, carry):
        i0 = pl.multiple_of(ig * SUBLANES, SUBLANES)
        v8 = [v_ref[s, pl.ds(i0, SUBLANES), :] for s in range(steps)]
        ys = [[] for _ in range(steps)]
        for ii in range(SUBLANES):
            s_i = s_scr[i0 + ii]
            for s in range(steps):
                sa = jnp.sum(s_i * kkn_ref[s], axis=0, keepdims=True)
                s_i = s_i * w_ref[s] + sa * b_ref[s] + v8[s][ii:ii + 1] * k_ref[s]
                ys[s].append(jnp.sum(s_i * r_ref[s], axis=0, keepdims=True))
            s_scr[i0 + ii] = s_i
        for s in range(steps):
            y_ref[s, pl.ds(i0, SUBLANES), :] = jnp.concatenate(ys[s], axis=0)
        return carry

    lax.fori_loop(0, RW_HEAD // SUBLANES, value_group, 0)

    @pl.when(ci == pl.num_programs(1) - 1)
    def _():
        st_ref[...] = s_scr[...]


def _wkv_scan_lanes(vecs, s0, *, tc):
    t, c, nl = vecs[0].shape
    g = c // RW_HEAD
    vec = pl.BlockSpec((tc, RW_HEAD, nl), lambda gi, ci: (ci, gi, 0))
    st = pl.BlockSpec((None, RW_HEAD, RW_HEAD, nl), lambda gi, ci: (gi, 0, 0, 0))
    return pl.pallas_call(
        functools.partial(_wkv_lanes_kernel, steps=tc),
        grid=(g, t // tc),
        in_specs=[vec] * 6 + [st],
        out_specs=[vec, st],
        out_shape=[jax.ShapeDtypeStruct((t, c, nl), F32), jax.ShapeDtypeStruct((g, RW_HEAD, RW_HEAD, nl), F32)],
        scratch_shapes=[pltpu.VMEM((RW_HEAD, RW_HEAD, nl), F32)],
        compiler_params=_params(("parallel", "arbitrary"), 48),
        name="wkv_scan_lanes",
    )(*vecs, s0)


def _rw_post_kernel(y_ref, g_ref, bonus_ref, lw_ref, lb_ref, seg_ref, segt_ref, o_ref):
    y = y_ref[...]
    seg = seg_ref[...]
    seg_t = segt_ref[...]
    inv = 1.0 / RW_HEAD
    mu = _seg_sum_bcast(y, seg, seg_t) * inv
    d = y - mu
    var = _seg_sum_bcast(d * d, seg, seg_t) * inv
    yn = d * lax.rsqrt(var + GN_EPS)
    o_ref[...] = ((yn * lw_ref[...] + lb_ref[...] + bonus_ref[...]) * g_ref[...]).astype(o_ref.dtype)


def _rw_post(y, g, bonus, lnx_w, lnx_b, seg, seg_t, *, tm):
    m, c = y.shape
    blk = pl.BlockSpec((tm, c), lambda i: (i, 0))
    vec = pl.BlockSpec((1, c), lambda i: (0, 0))
    return pl.pallas_call(
        _rw_post_kernel,
        grid=(m // tm,),
        in_specs=[blk, blk, blk, vec, vec, pl.BlockSpec((c, LANES), lambda i: (0, 0)),
                  pl.BlockSpec((LANES, c), lambda i: (0, 0))],
        out_specs=blk,
        out_shape=jax.ShapeDtypeStruct((m, c), BF16),
        compiler_params=_params(("parallel",), 48),
        name="rw_post",
    )(y, g, bonus, lnx_w, lnx_b, seg, seg_t)


QB = 128


def _attn_prompt_kernel(sl_ref, q0, k0, v0, q1, k1, v1, q2, k2, v2, out_ref, o_scr, l_scr, *, t):
    h = pl.program_id(1)
    scale = ATT_HEAD_DIM ** -0.5
    iq = lax.broadcasted_iota(jnp.int32, (QB, QB), 0)
    ik = lax.broadcasted_iota(jnp.int32, (QB, QB), 1)
    d_cur = (iq - ik).astype(F32)
    refs = ((q0, k0, v0), (q1, k1, v1), (q2, k2, v2))
    for g, (win, dil) in enumerate(ATT_GROUPS):
        q_ref, k_ref, v_ref = refs[g]
        slope = sl_ref[g * ATT_HEADS + h]
        nblk = t // dil // QB
        bias_cur = jnp.where(ik <= iq, -slope * dil * d_cur, NEG_INF)
        bias_prev = jnp.where(ik >= iq, -slope * dil * (d_cur + QB), NEG_INF)

        def block(idx, carry, q_ref=q_ref, k_ref=k_ref, v_ref=v_ref, dil=dil, nblk=nblk, g=g,
                  bias_cur=bias_cur, bias_prev=bias_prev):
            res = idx // nblk
            qb = idx % nblk
            base = res + qb * (QB * dil)
            pbase = res + jnp.maximum(qb - 1, 0) * (QB * dil)
            rows = pl.ds(base, QB, stride=dil) if dil > 1 else pl.ds(base, QB)
            prows = pl.ds(pbase, QB, stride=dil) if dil > 1 else pl.ds(pbase, QB)
            q = q_ref[rows, :].astype(BF16)
            kc = k_ref[rows, :].astype(BF16)
            dims = (((1,), (1,)), ((), ()))
            s_c = lax.dot_general(q, kc, dims, preferred_element_type=F32) * scale + bias_cur
            m = jnp.max(s_c, axis=-1, keepdims=True)
            if nblk > 1:
                kp = k_ref[prows, :].astype(BF16)
                s_p = lax.dot_general(q, kp, dims, preferred_element_type=F32) * scale + jnp.where(
                    qb > 0, bias_prev, NEG_INF)
                m = jnp.maximum(m, jnp.max(s_p, axis=-1, keepdims=True))
            p_c = jnp.exp(s_c - m)
            l = jnp.sum(p_c, axis=-1, keepdims=True)
            o = jnp.dot(p_c.astype(BF16), v_ref[rows, :].astype(BF16), preferred_element_type=F32)
            if nblk > 1:
                p_p = jnp.exp(s_p - m)
                l = l + jnp.sum(p_p, axis=-1, keepdims=True)
                o = o + jnp.dot(p_p.astype(BF16), v_ref[prows, :].astype(BF16), preferred_element_type=F32)
            o_scr[g, rows, :] = o / l
            l_scr[g, rows, :] = jnp.broadcast_to(m + jnp.log(l), (QB, ATT_HEAD_DIM))
            return carry

        lax.fori_loop(0, dil * nblk, block, 0, unroll=4)

    l0, l1, l2 = l_scr[0], l_scr[1], l_scr[2]
    m = jnp.maximum(jnp.maximum(l0, l1), l2)
    e0, e1, e2 = jnp.exp(l0 - m), jnp.exp(l1 - m), jnp.exp(l2 - m)
    den = e0 + e1 + e2
    out_ref[...] = ((e0 * o_scr[0] + e1 * o_scr[1] + e2 * o_scr[2]) / den).astype(out_ref.dtype)


def _attn_prompt(z, slopes, *, nb, t):
    specs = []
    for g in range(len(ATT_GROUPS)):
        for c in range(3):
            cb = (g * 3 + c) * ATT_HEADS
            specs.append(pl.BlockSpec((t, ATT_HEAD_DIM), lambda b, h, sl, cb=cb: (b, cb + h)))
    grid_spec = pltpu.PrefetchScalarGridSpec(
        num_scalar_prefetch=1,
        grid=(nb, ATT_HEADS),
        in_specs=specs,
        out_specs=pl.BlockSpec((t, ATT_HEAD_DIM), lambda b, h, sl: (b, h)),
        scratch_shapes=[pltpu.VMEM((3, t, ATT_HEAD_DIM), F32), pltpu.VMEM((3, t, ATT_HEAD_DIM), F32)],
    )
    return pl.pallas_call(
        functools.partial(_attn_prompt_kernel, t=t),
        grid_spec=grid_spec,
        out_shape=jax.ShapeDtypeStruct((nb * t, ATT_HEADS * ATT_HEAD_DIM), BF16),
        compiler_params=_params(("parallel", "arbitrary"), 48),
        name="attn_prompt",
    )(slopes, *([z] * 9))


N_NEW = 8
N_BAND = 128


def _attn_sample_kernel(q0, q1, q2, n0, n1, n2, c0, c1, c2, sl_ref, out_ref, sc_scr, o_scr, l_scr):
    scale = ATT_HEAD_DIM ** -0.5 * LOG2_E
    qs, news, caches = (q0, q1, q2), (n0, n1, n2), (c0, c1, c2)
    tile = (ATT_HEADS, ATT_HEAD_DIM)

    def group(g):
        win, dil = ATT_GROUPS[g]
        slot = g % 2
        q_ref, n_ref, c_ref = qs[g], news[g], caches[g]
        slope = sl_ref[g] * LOG2_E
        past = N_BAND * dil
        qv = [q_ref[s] * scale for s in range(N_NEW)]
        if dil == 1:
            combos = [(0, s) for s in range(N_NEW)]
        elif dil == 4:
            combos = [(r, s) for r in range(4) for s in (r, r + 4)]
        else:
            combos = [(r, r) for r in range(N_NEW)]

        def row_tiles(m, r):
            if dil == 1:
                return c_ref[m, 0], c_ref[m, 1]
            if dil == 4:
                a = m * 4 + r
                return c_ref[a, 0], c_ref[a, 1]
            return c_ref[m, r, 0], c_ref[m, r, 1]

        def score(m, r, s):
            a = m * dil + r
            k_t, _ = row_tiles(m, r)
            sc = jnp.sum(qv[s] * k_t, axis=-1, keepdims=True) - slope * jnp.asarray(past + s - a, F32)
            if dil == 1:
                sc = jnp.where(m >= s, sc, NEG_INF)
            elif dil == 4 and s >= 4:
                sc = jnp.where(m >= 1, sc, NEG_INF)
            return jnp.broadcast_to(sc, tile)

        new_sc = []
        for s in range(N_NEW):
            lst = []
            for j in range(s // dil + 1):
                s2 = s - j * dil
                sc = jnp.sum(qv[s] * n_ref[s2, 0], axis=-1, keepdims=True) - slope * float(j * dil)
                lst.append((s2, jnp.broadcast_to(sc, tile)))
            new_sc.append(lst)
        m0 = [functools.reduce(jnp.maximum, [x for _, x in new_sc[s]]) for s in range(N_NEW)]

        def pass1(m, mx):
            mx = list(mx)
            for ci, (r, s) in enumerate(combos):
                sc = score(m, r, s)
                sc_scr[slot, m, ci] = sc
                mx[s] = jnp.maximum(mx[s], sc)
            return tuple(mx)

        def start2(mx):
            l_init, o_init = [], []
            for s in range(N_NEW):
                l_s = jnp.zeros(tile, F32)
                o_s = jnp.zeros(tile, F32)
                for s2, sc in new_sc[s]:
                    p = jnp.exp2(sc - mx[s])
                    l_s = l_s + p
                    o_s = o_s + p * n_ref[s2, 1]
                l_init.append(l_s)
                o_init.append(o_s)
            return tuple(l_init), tuple(o_init)

        def pass2(m, carry, mx):
            ls, os_ = list(carry[0]), list(carry[1])
            for ci, (r, s) in enumerate(combos):
                _, v_t = row_tiles(m, r)
                p = jnp.exp2(sc_scr[slot, m, ci] - mx[s])
                ls[s] = ls[s] + p
                os_[s] = os_[s] + p * v_t
            return tuple(ls), tuple(os_)

        def finish(carry, mx):
            ls, os_ = carry
            for s in range(N_NEW):
                o_scr[g, s] = os_[s] / ls[s]
                l_scr[g, s] = mx[s] * (1.0 / LOG2_E) + jnp.log(ls[s])

        return tuple(m0), pass1, start2, pass2, finish

    m0, pass1, start2, pass2, finish = group(0)
    mx_prev = lax.fori_loop(0, N_BAND, pass1, m0, unroll=32)
    for g in range(1, len(ATT_GROUPS)):
        m0, pass1, start2_next, pass2_next, finish_next = group(g)

        def fused(m, carry, pass1=pass1, pass2=pass2, mx_prev=mx_prev):
            return pass1(m, carry[0]), pass2(m, carry[1], mx_prev)

        mx_new, acc = lax.fori_loop(0, N_BAND, fused, (m0, start2(mx_prev)), unroll=16)
        finish(acc, mx_prev)
        mx_prev, start2, pass2, finish = mx_new, start2_next, pass2_next, finish_next
    acc = lax.fori_loop(0, N_BAND, lambda m, carry: pass2(m, carry, mx_prev), start2(mx_prev), unroll=8)
    finish(acc, mx_prev)

    for s in range(N_NEW):
        l0, l1, l2 = l_scr[0, s], l_scr[1, s], l_scr[2, s]
        m = jnp.maximum(jnp.maximum(l0, l1), l2)
        e0, e1, e2 = jnp.exp(l0 - m), jnp.exp(l1 - m), jnp.exp(l2 - m)
        out_ref[s] = (e0 * o_scr[0, s] + e1 * o_scr[1, s] + e2 * o_scr[2, s]) / (e0 + e1 + e2)


def _attn_sample(qs, news, caches, slope_tiles):
    nb = qs[0].shape[0]
    he = (ATT_HEADS, ATT_HEAD_DIM)
    qspec = pl.BlockSpec((None, N_NEW) + he, lambda b: (b, 0, 0, 0))
    nspec = pl.BlockSpec((None, N_NEW, 2) + he, lambda b: (b, 0, 0, 0, 0))
    c2 = caches[2]
    dil2 = ATT_GROUPS[2][1]
    c2v = c2.reshape(nb, c2.shape[1] // dil2, dil2, 2, *he)
    cspecs = [
        pl.BlockSpec((None, caches[0].shape[1], 2) + he, lambda b: (b, 0, 0, 0, 0)),
        pl.BlockSpec((None, caches[1].shape[1], 2) + he, lambda b: (b, 0, 0, 0, 0)),
        pl.BlockSpec((None, N_BAND, N_NEW, 2) + he, lambda b: (b, 0, 0, 0, 0, 0)),
    ]
    return pl.pallas_call(
        _attn_sample_kernel,
        grid=(nb,),
        in_specs=[qspec] * 3 + [nspec] * 3 + cspecs + [pl.BlockSpec((3,) + he, lambda b: (0, 0, 0))],
        out_specs=pl.BlockSpec((None, N_NEW) + he, lambda b: (b, 0, 0, 0)),
        out_shape=jax.ShapeDtypeStruct((nb, N_NEW) + he, F32),
        scratch_shapes=[pltpu.VMEM((2, N_BAND, N_NEW) + he, F32), pltpu.VMEM((3, N_NEW) + he, F32),
                        pltpu.VMEM((3, N_NEW) + he, F32)],
        compiler_params=_params(("parallel",), 48),
        name="attn_sample",
    )(*qs, *news, caches[0], caches[1], c2v, slope_tiles)


def kernel(x_prompt, x_sample, cache_kv_0, cache_kv_1, cache_kv_2, state_wkv, state_shift, norm_mix, w_in, rw_mu,
           rw_w0, rw_w_up, rw_a0, rw_a_up, rw_g_up, rw_k_k, rw_k_a, rw_r_k, rw_lnx_w, rw_lnx_b, rw_out, att_out,
           w_o, norm_ffn, w_ffn_in, w_ffn_out, norm_final):
    depth = w_in.shape[0]
    assert depth == 1, "single-layer stack"
    bp, tp, d = x_prompt.shape
    bs, ts, _ = x_sample.shape
    mp, ms = bp * tp, bs * ts
    assert ts == N_NEW
    c = rw_w0.shape[1]
    r_dec, r_aaa, r_gate = rw_w_up.shape[1], rw_a_up.shape[1], rw_g_up.shape[1]
    rw_cols = 3 * c + r_dec + r_aaa + r_gate
    att_cols = len(ATT_GROUPS) * 3 * ATT_HEADS * ATT_HEAD_DIM
    att_w = ATT_HEADS * ATT_HEAD_DIM
    pad = (-(att_cols + rw_cols)) % LANES
    rwp = rw_cols + pad
    assert rwp - 3 * c == 512
    nz = att_cols + rwp

    tail = lambda w, off: jnp.zeros((512, c), F32).at[off:off + w.shape[0]].set(w).astype(BF16)
    row = lambda v, n: jnp.pad(v.reshape(1, -1), ((0, 0), (0, n - v.size)))
    seg = (jnp.arange(c // 2)[:, None] // RW_HEAD == jnp.arange(LANES)[None, :]).astype(BF16)
    prm = {
        "mu": row(rw_mu[0], rwp), "w0": row(rw_w0[0], c), "a0": row(rw_a0[0], c),
        "wup": tail(rw_w_up[0], 0), "aup": tail(rw_a_up[0], r_dec), "gup": tail(rw_g_up[0], r_dec + r_aaa),
        "k_k": row(rw_k_k[0], c), "k_a": row(rw_k_a[0], c), "r_k": row(rw_r_k[0].reshape(-1), c),
        "seg": seg, "seg_t": seg.T,
    }
    seg_full = (jnp.arange(c)[:, None] // RW_HEAD == jnp.arange(LANES)[None, :]).astype(BF16)

    x_rows = (x_prompt.reshape(mp, d), x_sample.reshape(ms, d))
    hmix = _rmsnorm_pair(*x_rows, norm_mix[0], BF16)
    wi_t = jnp.swapaxes(w_in, 1, 2)[0]
    z = _matmul(hmix, wi_t, tm=1024, tn=512, n_out=nz, transposed=True, vmem_mib=56, name="w_in")
    zg = _matmul(hmix, wi_t[att_cols + rw_cols:], tm=1024, tn=512, transposed=True, vmem_mib=56,
                 name="w_gate")
    z_s = z[mp:].reshape(bs, ts, nz)

    new_kv_p, new_kv_s = [], []
    for g, (win, dil) in enumerate(ATT_GROUPS):
        lo = g * 3 * att_w + att_w
        keep = min(win, tp)
        kv = jnp.stack([z[(b + 1) * tp - keep:(b + 1) * tp, lo:lo + 2 * att_w] for b in range(bp)])
        new_kv_p.append(kv.reshape(bp, keep, 2, ATT_HEADS, ATT_HEAD_DIM)[None])
        new_kv_s.append(z_s[:, :, lo:lo + 2 * att_w].reshape(bs, ts, 2, ATT_HEADS, ATT_HEAD_DIM)[None])
    shift_p = jnp.stack([z[(b + 1) * tp - 1, att_cols:att_cols + rw_cols] for b in range(bp)])[None]
    shift_s = z_s[:, -1, att_cols:att_cols + rw_cols][None]

    tm_prep = 256
    prep_p = _rwkv_prep(z, 0, mp, att_cols, None, jnp.zeros((bp, 1, rwp), F32), tp, prm, tm=tm_prep)
    cols_s = z_s[:, :, att_cols:att_cols + rwp]
    shift_in = jnp.pad(state_shift[0], ((0, 0), (0, pad)))
    prev_s = jnp.concatenate([shift_in[:, None, :], cols_s[:, :-1]], axis=1).reshape(ms, rwp)
    prep_s = _rwkv_prep(z, mp, ms, att_cols, prev_s, None, ts, prm, tm=tm_prep)

    h_rw = c // RW_HEAD
    y_p, wkv_p = _wkv_scan(*prep_p[:6], jnp.zeros((1, bp, h_rw, RW_HEAD, RW_HEAD), F32), nb=bp, t=tp, tc=64)
    to_lanes_s = lambda a: a.reshape(bs, ts, c).transpose(1, 2, 0)
    y_t, wkv_t = _wkv_scan_lanes([to_lanes_s(a) for a in prep_s[:6]], jnp.transpose(state_wkv[0], (1, 2, 3, 0)),
                                 tc=ts)
    y_s = y_t.transpose(2, 0, 1).reshape(ms, c)
    wkv_s = jnp.transpose(wkv_t, (3, 0, 1, 2))[None]
    lw, lb = rw_lnx_w[0].reshape(1, c), rw_lnx_b[0].reshape(1, c)
    rw_p = _rw_post(y_p, prep_p[6], prep_p[7], lw, lb, seg_full, seg_full.T, tm=256)
    rw_s = _rw_post(y_s, prep_s[6], prep_s[7], lw, lb, seg_full, seg_full.T, tm=256)
    rw_act = jnp.concatenate([rw_p, rw_s], axis=0)

    n_heads_all = len(ATT_GROUPS) * ATT_HEADS
    slopes = 2.0 ** (-8.0 * jnp.arange(1, n_heads_all + 1, dtype=F32) / n_heads_all)
    att_p = _attn_prompt(z, slopes, nb=bp, t=tp)
    qs = [z_s[:, :, g * 3 * att_w:g * 3 * att_w + att_w].reshape(bs, ts, ATT_HEADS, ATT_HEAD_DIM)
          for g in range(len(ATT_GROUPS))]
    slope_tiles = jnp.broadcast_to(slopes.reshape(len(ATT_GROUPS), ATT_HEADS, 1),
                                   (len(ATT_GROUPS), ATT_HEADS, ATT_HEAD_DIM))
    att_s = _attn_sample(qs, [kv[0] for kv in new_kv_s], [cache_kv_0[0], cache_kv_1[0], cache_kv_2[0]], slope_tiles)
    att_act = jnp.concatenate([att_p, att_s.reshape(ms, att_w).astype(BF16)], axis=0)

    mixed = _mix(rw_act, att_act, rw_out[0].astype(BF16), att_out[0].astype(BF16), zg, 0, tm=1024, tn=512)
    x1 = _matmul(mixed, w_o[0].astype(BF16), tm=1024, tn=512, residual=x_rows, name="w_o")
    hf = _rmsnorm(x1, norm_ffn[0], BF16)
    act = _swiglu(hf, w_ffn_in[0], tm=1024, tn=256)
    x2 = _matmul(act, w_ffn_out[0].astype(BF16), tm=512, tn=256, residual=x1, vmem_mib=56, name="w_ffn_out")
    y_p = _rmsnorm(x2, norm_final, F32, 0, mp)
    y_s = _rmsnorm(x2, norm_final, F32, mp, ms)

    return (y_p.reshape(bp, tp, d), y_s.reshape(bs, ts, d),
            new_kv_p[0], new_kv_p[1], new_kv_p[2], wkv_p, shift_p,
            new_kv_s[0], new_kv_s[1], new_kv_s[2], wkv_s, shift_s)
```
